```python
import math
import jax
import jax.numpy as jnp
from jax import lax
import numpy as np

D_MODEL = 1024
BATCH = 16
SEQ = 2048
DEPTH = 1
DEC_BATCH = 128
DEC_SEQ = 1
PAST_LEN = 8192
PAGE_SIZE = 128

HEAD_DIM = 64
H_A = 8
H_B = 8
W_A = H_A * HEAD_DIM
W_B = H_B * HEAD_DIM
H_IDX = 8
D_IDX = 64
TOPK_MAX = 256
N_BUCKETS = 32
MAX_DISTANCE = 128
Q_BLOCK = 128
FORGET_BIAS_INIT = 3.0
EPS = 1e-6
NEG = -1e30
SCALE = HEAD_DIM ** -0.5
IDX_W_SCALE = (H_IDX * D_IDX) ** -0.5
PROJ_WIDTH = 4 * W_A + H_IDX * D_IDX + H_IDX + D_IDX + 4 * W_B + H_B + 2 * D_MODEL

kernel_name = "hybrid_dsa_fox_gated_decode_step"


def _partition():
    widths = (W_A, W_A, W_A, W_A, H_IDX * D_IDX, H_IDX, D_IDX,
              W_B, W_B, W_B, W_B, H_B, D_MODEL, D_MODEL)
    offs, acc = [], 0
    for w in widths[:-1]:
        acc += w
        offs.append(acc)
    return offs


def _rms(x, g):
    xf = x.astype(jnp.float32)
    y = xf * lax.rsqrt(jnp.mean(xf * xf, axis=-1, keepdims=True) + EPS)
    return (y * g.astype(jnp.float32)).astype(x.dtype)


def _t5_bucket(dist):
    max_exact = N_BUCKETS // 2
    d = jnp.maximum(dist, 0)
    df = jnp.maximum(d, 1).astype(jnp.float32)
    large = max_exact + (jnp.log(df / max_exact) / math.log(MAX_DISTANCE / max_exact)
                         * (N_BUCKETS - max_exact)).astype(jnp.int32)
    large = jnp.minimum(large, N_BUCKETS - 1)
    return jnp.where(d < max_exact, d, large)


def _branch_inputs(x, g_norm, w_in, b_f, qn_a, kn_a, qn_b, kn_b):
    lead = x.shape[:2]
    h = _rms(x, g_norm)
    p = jnp.einsum('bsd,de->bse', h, w_in)
    qa, ka, va, za, iq, iw, ik, qb, kb, vb, zb, fl, ga, gb = jnp.split(p, _partition(), axis=-1)
    qa = _rms(qa.reshape(*lead, H_A, HEAD_DIM), qn_a)
    ka = _rms(ka.reshape(*lead, H_A, HEAD_DIM), kn_a)
    va = va.reshape(*lead, H_A, HEAD_DIM)
    qb = _rms(qb.reshape(*lead, H_B, HEAD_DIM), qn_b)
    kb = _rms(kb.reshape(*lead, H_B, HEAD_DIM), kn_b)
    vb = vb.reshape(*lead, H_B, HEAD_DIM)
    iq = iq.reshape(*lead, H_IDX, D_IDX)
    iw = iw * IDX_W_SCALE
    logf = jax.nn.log_sigmoid(fl.astype(jnp.float32) + b_f.astype(jnp.float32))
    return qa, ka, va, za, iq, iw, ik, qb, kb, vb, zb, logf, ga, gb


def _index_scores(iq, iw, ik, pos_q, pos_k):
    r = jax.nn.relu(jnp.einsum('...thi,...si->...ths', iq.astype(jnp.float32), ik.astype(jnp.float32)))
    s = jnp.einsum('...th,...ths->...ts', iw.astype(jnp.float32), r)
    return jnp.where(pos_k[None, :] <= pos_q[:, None], s, -jnp.inf)


def _gathered_attention(q, kg, vg, sel, pos_q, rel_bias):
    f32 = jnp.float32
    dist = pos_q - sel
    logits = jnp.einsum('...thd,...tkhd->...thk', q.astype(f32), kg.astype(f32)) * SCALE
    bias = rel_bias.astype(f32)[_t5_bucket(dist)]
    logits = logits + jnp.swapaxes(bias, -1, -2)
    logits = jnp.where((dist >= 0)[..., None, :], logits, NEG)
    p = jax.nn.softmax(logits, axis=-1)
    return jnp.einsum('...thk,...tkhd->...thd', p, vg.astype(f32))


def _dsa_prompt(qa, ka, va, iq, iw, ik, rel_bias):
    B, S = qa.shape[:2]
    nb = S // Q_BLOCK
    topk = min(TOPK_MAX, S // 4)
    pos_k = jnp.arange(S)

    def one(item):
        b = item // nb
        q0 = (item % nb) * Q_BLOCK
        pos_q = q0 + jnp.arange(Q_BLOCK)
        q = lax.dynamic_slice(qa, (b, q0, 0, 0), (1, Q_BLOCK, H_A, HEAD_DIM))[0]
        iqb = lax.dynamic_slice(iq, (b, q0, 0, 0), (1, Q_BLOCK, H_IDX, D_IDX))[0]
        iwb = lax.dynamic_slice(iw, (b, q0, 0), (1, Q_BLOCK, H_IDX))[0]
        score = _index_scores(iqb, iwb, ik[b], pos_q, pos_k)
        _, sel = lax.top_k(score, topk)
        kg = ka[b, sel]
        vg = va[b, sel]
        return _gathered_attention(q, kg, vg, sel, pos_q[:, None], rel_bias)

    o = lax.map(one, jnp.arange(B * nb))
    return o.reshape(B, S, H_A, HEAD_DIM)


def _fox_prompt(qb, kb, vb, logf):
    B, S = qb.shape[:2]
    nb = S // Q_BLOCK
    f32 = jnp.float32
    Ft = jnp.swapaxes(jnp.cumsum(logf, axis=1), 1, 2)
    k = kb.astype(f32)
    v = vb.astype(f32)
    pos_k = jnp.arange(S)

    def blk(i):
        q0 = i * Q_BLOCK
        q = lax.dynamic_slice_in_dim(qb, q0, Q_BLOCK, 1).astype(f32)
        Fq = lax.dynamic_slice_in_dim(Ft, q0, Q_BLOCK, 2)
        s = jnp.einsum('bthd,bshd->bhts', q, k) * SCALE + (Fq[..., None] - Ft[:, :, None, :])
        mask = pos_k[None, :] <= (q0 + jnp.arange(Q_BLOCK))[:, None]
        p = jax.nn.softmax(jnp.where(mask, s, NEG), axis=-1)
        return jnp.einsum('bhts,bshd->bthd', p, v)

    o = lax.map(blk, jnp.arange(nb))
    return jnp.transpose(o, (1, 0, 2, 3, 4)).reshape(B, S, H_B, HEAD_DIM)


def _dsa_sample(layer, qa, ka, va, iq, iw, ik, cache_k, cache_v, cache_ik, page_table, rel_bias):
    Bd, T = qa.shape[:2]
    L = PAST_LEN + T
    topk = min(TOPK_MAX, L // 4)
    f32 = jnp.float32
    pos_q = PAST_LEN + jnp.arange(T)
    pos_k = jnp.arange(L)
    ik_past = cache_ik[layer, page_table].reshape(Bd, PAST_LEN, D_IDX)
    ik_all = jnp.concatenate([ik_past.astype(f32), ik.astype(f32)], axis=1)
    score = _index_scores(iq, iw, ik_all, pos_q, pos_k)
    _, sel = lax.top_k(score, topk)
    in_past = (sel < PAST_LEN)[..., None, None]
    sel_p = jnp.minimum(sel, PAST_LEN - 1)
    bidx = jnp.arange(Bd)[:, None, None]
    phys = page_table[bidx, sel_p // PAGE_SIZE]
    off = sel_p % PAGE_SIZE
    sel_n = jnp.clip(sel - PAST_LEN, 0, T - 1)
    kg = jnp.where(in_past, cache_k[layer, phys, off].astype(f32), ka[bidx, sel_n].astype(f32))
    vg = jnp.where(in_past, cache_v[layer, phys, off].astype(f32), va[bidx, sel_n].astype(f32))
    return _gathered_attention(qa, kg, vg, sel, pos_q[:, None], rel_bias)


def _fox_sample(layer, qb, kb, vb, logf, cache_k, cache_v, cache_logf, page_table):
    Bd, T = qb.shape[:2]
    n_pages = PAST_LEN // PAGE_SIZE
    f32 = jnp.float32
    q = qb.astype(f32)
    lf_past = cache_logf[layer, page_table].astype(f32).reshape(Bd, PAST_LEN, H_B)
    suffix = jnp.swapaxes(lax.cumsum(lf_past, axis=1, reverse=True) - lf_past, 1, 2)
    fn = jnp.swapaxes(jnp.cumsum(logf, axis=1), 1, 2)

    def step(carry, j):
        m, l, acc = carry
        phys = page_table[:, j]
        k = cache_k[layer, phys].astype(f32)
        v = cache_v[layer, phys].astype(f32)
        suf = lax.dynamic_slice_in_dim(suffix, j * PAGE_SIZE, PAGE_SIZE, axis=2)
        s = jnp.einsum('bthd,bshd->bhts', q, k) * SCALE + fn[..., :, None] + suf[:, :, None, :]
        m_new = jnp.maximum(m, jnp.max(s, axis=-1))
        p = jnp.exp(s - m_new[..., None])
        corr = jnp.exp(m - m_new)
        return (m_new, l * corr + jnp.sum(p, axis=-1),
                acc * corr[..., None] + jnp.einsum('bhts,bshd->bhtd', p, v)), None

    init = (jnp.full((Bd, H_B, T), NEG, f32), jnp.zeros((Bd, H_B, T), f32),
            jnp.zeros((Bd, H_B, T, HEAD_DIM), f32))
    (m, l, acc), _ = lax.scan(step, init, jnp.arange(n_pages))
    s = jnp.einsum('bthd,bshd->bhts', q, kb.astype(f32)) * SCALE + (fn[..., :, None] - fn[..., None, :])
    causal = jnp.arange(T)[None, :] <= jnp.arange(T)[:, None]
    s = jnp.where(causal, s, NEG)
    m_new = jnp.maximum(m, jnp.max(s, axis=-1))
    p = jnp.exp(s - m_new[..., None])
    corr = jnp.exp(m - m_new)
    l = l * corr + jnp.sum(p, axis=-1)
    acc = acc * corr[..., None] + jnp.einsum('bhts,bshd->bhtd', p, vb.astype(f32))
    return jnp.swapaxes(acc / l[..., None], 1, 2)


def _merge(x, oa, za, ob, zb, ga, gb, w_up_a, w_up_b, w_out):
    lead = x.shape[:2]
    ua = oa.reshape(*lead, W_A).astype(x.dtype) * jax.nn.silu(za)
    ub = ob.reshape(*lead, W_B).astype(x.dtype) * jax.nn.silu(zb)
    ya = jnp.einsum('bsw,wd->bsd', ua, w_up_a)
    yb = jnp.einsum('bsw,wd->bsd', ub, w_up_b)
    m = jax.nn.sigmoid(ga) * ya + jax.nn.sigmoid(gb) * yb
    return x + jnp.einsum('bsd,de->bse', m, w_out)


def setup_inputs(seed: int = 0) -> dict:
    key = jax.random.key(seed)
    ks = jax.random.split(key, 24)
    f32 = jnp.float32
    n_pages = PAST_LEN // PAGE_SIZE
    n_pool = (DEC_BATCH * n_pages * 5) // 4

    def nrm(k, shape, scale=1.0):
        return scale * jax.random.normal(k, shape, f32)

    x_prompt = nrm(ks[0], (BATCH, SEQ, D_MODEL))
    x_sample = nrm(ks[1], (DEC_BATCH, DEC_SEQ, D_MODEL))
    cache_a_k = nrm(ks[2], (DEPTH, n_pool, PAGE_SIZE, H_A, HEAD_DIM))
    cache_a_v = nrm(ks[3], (DEPTH, n_pool, PAGE_SIZE, H_A, HEAD_DIM))
    cache_a_idx_k = nrm(ks[4], (DEPTH, n_pool, PAGE_SIZE, D_IDX))
    cache_b_k = nrm(ks[5], (DEPTH, n_pool, PAGE_SIZE, H_B, HEAD_DIM))
    cache_b_v = nrm(ks[6], (DEPTH, n_pool, PAGE_SIZE, H_B, HEAD_DIM))
    cache_b_logf = jax.nn.log_sigmoid(FORGET_BIAS_INIT + nrm(ks[7], (DEPTH, n_pool, PAGE_SIZE, H_B)))
    page_table = jax.random.permutation(ks[8], n_pool)[: DEC_BATCH * n_pages].reshape(
        DEC_BATCH, n_pages).astype(jnp.int32)
    rel_bias = nrm(ks[9], (N_BUCKETS, H_A), 0.5)
    g_norm = 1.0 + nrm(ks[10], (DEPTH, D_MODEL), 0.05)
    w_in = nrm(ks[11], (DEPTH, D_MODEL, PROJ_WIDTH), D_MODEL ** -0.5)
    b_fgate = FORGET_BIAS_INIT + nrm(ks[12], (DEPTH, H_B), 0.5)
    qnorm_a = 1.0 + nrm(ks[13], (DEPTH, HEAD_DIM), 0.05)
    knorm_a = 1.0 + nrm(ks[14], (DEPTH, HEAD_DIM), 0.05)
    qnorm_b = 1.0 + nrm(ks[15], (DEPTH, HEAD_DIM), 0.05)
    knorm_b = 1.0 + nrm(ks[16], (DEPTH, HEAD_DIM), 0.05)
    w_up_a = nrm(ks[17], (DEPTH, W_A, D_MODEL), W_A ** -0.5)
    w_up_b = nrm(ks[18], (DEPTH, W_B, D_MODEL), W_B ** -0.5)
    w_out = nrm(ks[19], (DEPTH, D_MODEL, D_MODEL), D_MODEL ** -0.5)
    return {"x_prompt": x_prompt, "x_sample": x_sample,
            "cache_a_k": cache_a_k, "cache_a_v": cache_a_v, "cache_a_idx_k": cache_a_idx_k,
            "cache_b_k": cache_b_k, "cache_b_v": cache_b_v, "cache_b_logf": cache_b_logf,
            "page_table": page_table, "rel_bias": rel_bias,
            "g_norm": g_norm, "w_in": w_in, "b_fgate": b_fgate,
            "qnorm_a": qnorm_a, "knorm_a": knorm_a, "qnorm_b": qnorm_b, "knorm_b": knorm_b,
            "w_up_a": w_up_a, "w_up_b": w_up_b, "w_out": w_out}


def reference(x_prompt, x_sample, cache_a_k, cache_a_v, cache_a_idx_k, cache_b_k, cache_b_v,
              cache_b_logf, page_table, rel_bias, g_norm, w_in, b_fgate, qnorm_a, knorm_a,
              qnorm_b, knorm_b, w_up_a, w_up_b, w_out):
    xp, xs = x_prompt, x_sample
    st_p, st_s = [], []
    for layer in range(DEPTH):
        wts = (g_norm[layer], w_in[layer], b_fgate[layer], qnorm_a[layer], knorm_a[layer],
               qnorm_b[layer], knorm_b[layer])
        qa, ka, va, za, iq, iw, ik, qb, kb, vb, zb, lf, ga, gb = _branch_inputs(xp, *wts)
        oa = _dsa_prompt(qa, ka, va, iq, iw, ik, rel_bias)
        ob = _fox_prompt(qb, kb, vb, lf)
        st_p.append((ka, va, ik, kb, vb, lf))
        xp = _merge(xp, oa, za, ob, zb, ga, gb, w_up_a[layer], w_up_b[layer], w_out[layer])
        qa, ka, va, za, iq, iw, ik, qb, kb, vb, zb, lf, ga, gb = _branch_inputs(xs, *wts)
        oa = _dsa_sample(layer, qa, ka, va, iq, iw, ik, cache_a_k, cache_a_v, cache_a_idx_k,
                         page_table, rel_bias)
        ob = _fox_sample(layer, qb, kb, vb, lf, cache_b_k, cache_b_v, cache_b_logf, page_table)
        st_s.append((ka, va, ik, kb, vb, lf))
        xs = _merge(xs, oa, za, ob, zb, ga, gb, w_up_a[layer], w_up_b[layer], w_out[layer])
    a_k_p, a_v_p, a_ik_p, b_k_p, b_v_p, b_lf_p = [jnp.stack(z) for z in zip(*st_p)]
    a_k_s, a_v_s, a_ik_s, b_k_s, b_v_s, b_lf_s = [jnp.stack(z) for z in zip(*st_s)]
    return (xp, xs, a_k_p, a_v_p, a_ik_p, b_k_p, b_v_p, b_lf_p,
            a_k_s, a_v_s, a_ik_s, b_k_s, b_v_s, b_lf_s)
```

```python
import functools
import math

import jax
import jax.numpy as jnp
from jax import lax
from jax.experimental import pallas as pl
from jax.experimental.pallas import tpu as pltpu

F32 = jnp.float32
BF16 = jnp.bfloat16
I32 = jnp.int32

HEAD_DIM = 64
N_HEADS = 8
HW = N_HEADS * HEAD_DIM
D_IDX = 64
PAGE = 128
TOPK_MAX = 256
N_BUCKETS = 32
MAX_EXACT = N_BUCKETS // 2
MAX_DISTANCE = 128
FAR_BUCKET = N_BUCKETS - 1
EPS = 1e-6
NEG = -1e30
SCALE = HEAD_DIM ** -0.5
IDX_W_SCALE = (N_HEADS * D_IDX) ** -0.5

LANES = 128
PAGES_PER_STEP = 8
QPAD = 16
N_BISECT = 22
VMEM_LIMIT = 56 * 1024 * 1024

NT = (((1,), (1,)), ((), ()))


def _cparams(n_axes, vmem=None):
    return pltpu.CompilerParams(dimension_semantics=("arbitrary",) * n_axes, vmem_limit_bytes=vmem)


def _log_sigmoid(z):
    return -(jnp.maximum(-z, 0.0) + jnp.log1p(jnp.exp(-jnp.abs(z))))


def _t5_bucket(dist):
    d = jnp.maximum(dist, 0)
    df = jnp.maximum(d, 1).astype(F32)
    large = MAX_EXACT + (jnp.log(df / MAX_EXACT) / math.log(MAX_DISTANCE / MAX_EXACT)
                         * (N_BUCKETS - MAX_EXACT)).astype(I32)
    large = jnp.minimum(large, N_BUCKETS - 1)
    return jnp.where(d < MAX_EXACT, d, large)


def _project_body(x_ref, g_ref, wa_ref, ws_ref, wb_ref, wg_ref, wft_ref, bf_ref, bft_ref, nrm_ref, gm_ref,
                  qa_o, ka_o, va_o, ka16_o, va16_o, sza_o, iq_o, iw_o, ik_o, ik16_o,
                  qb_o, kb_o, vb_o, kb16_o, vb16_o, szb_o, lf_o, lft_o, sga_o, sgb_o, *, d_model):
    x = x_ref[...]
    ms = jnp.mean(x * x, axis=-1, keepdims=True)
    h = ((x * lax.rsqrt(ms + EPS)) * g_ref[...]).astype(BF16)
    gm = gm_ref[...]

    def seg(w_ref, i):
        return jnp.dot(h, w_ref[:, i * HW:(i + 1) * HW], preferred_element_type=F32)

    def head_norm(p, row):
        ss = jnp.dot((p * p).astype(BF16), gm, preferred_element_type=F32)
        return (p * lax.rsqrt(ss + EPS)) * nrm_ref[row:row + 1, :]

    def silu(z):
        return z * jax.nn.sigmoid(z)

    qa_o[...] = (head_norm(seg(wa_ref, 0), 0) * SCALE).astype(BF16)
    ka = head_norm(seg(wa_ref, 1), 1)
    ka_o[...] = ka
    ka16_o[...] = ka.astype(BF16)
    va = seg(wa_ref, 2)
    va_o[...] = va
    va16_o[...] = va.astype(BF16)
    sza_o[...] = silu(seg(wa_ref, 3)).astype(BF16)
    iq_o[...] = seg(wa_ref, 4).astype(BF16)

    ps = jnp.dot(h, ws_ref[...], preferred_element_type=F32)
    ik = ps[:, 0:D_IDX]
    ik_o[...] = ik
    ik16_o[...] = ik.astype(BF16)
    iw_o[...] = ps[:, D_IDX:D_IDX + N_HEADS] * IDX_W_SCALE
    lf_o[...] = _log_sigmoid(ps[:, D_IDX + N_HEADS:D_IDX + 2 * N_HEADS] + bf_ref[...])
    flt = lax.dot_general(wft_ref[...], h, NT, preferred_element_type=F32)
    lft_o[...] = _log_sigmoid(flt + bft_ref[...])

    qb_o[...] = (head_norm(seg(wb_ref, 0), 2) * SCALE).astype(BF16)
    kb = head_norm(seg(wb_ref, 1), 3)
    kb_o[...] = kb
    kb16_o[...] = kb.astype(BF16)
    vb = seg(wb_ref, 2)
    vb_o[...] = vb
    vb16_o[...] = vb.astype(BF16)
    szb_o[...] = silu(seg(wb_ref, 3)).astype(BF16)

    for c in range(d_model // HW):
        sga_o[:, c * HW:(c + 1) * HW] = jax.nn.sigmoid(seg(wg_ref, c)).astype(BF16)
        sgb_o[:, c * HW:(c + 1) * HW] = jax.nn.sigmoid(seg(wg_ref, d_model // HW + c)).astype(BF16)


def _project(x2d, wts):
    rows, d_model = x2d.shape
    tm = min(256, rows)
    assert rows % tm == 0 and d_model % HW == 0
    consts = (wts["g"], wts["wa"], wts["ws"], wts["wb"], wts["wg"], wts["wft"], wts["bf"], wts["bft"],
              wts["nrm"], wts["gm"])

    def row(n, dt):
        return jax.ShapeDtypeStruct((rows, n), dt), pl.BlockSpec((tm, n), lambda i: (i, 0))

    outs = [row(HW, BF16), row(HW, F32), row(HW, F32), row(HW, BF16), row(HW, BF16), row(HW, BF16),
            row(HW, BF16), row(N_HEADS, F32), row(D_IDX, F32), row(D_IDX, BF16),
            row(HW, BF16), row(HW, F32), row(HW, F32), row(HW, BF16), row(HW, BF16), row(HW, BF16),
            row(N_HEADS, F32),
            (jax.ShapeDtypeStruct((N_HEADS, rows), F32), pl.BlockSpec((N_HEADS, tm), lambda i: (0, i))),
            row(d_model, BF16), row(d_model, BF16)]
    names = ("qa16", "ka", "va", "ka16", "va16", "sza", "iq16", "iw", "ik", "ik16",
             "qb16", "kb", "vb", "kb16", "vb16", "szb", "lf", "lft", "sga", "sgb")
    res = pl.pallas_call(
        functools.partial(_project_body, d_model=d_model),
        grid=(rows // tm,),
        in_specs=[pl.BlockSpec((tm, d_model), lambda i: (i, 0))]
        + [pl.BlockSpec(c.shape, lambda i, n=c.ndim: (0,) * n) for c in consts],
        out_specs=[o[1] for o in outs],
        out_shape=[o[0] for o in outs],
        compiler_params=_cparams(1, VMEM_LIMIT),
        name="project",
    )(x2d, *consts)
    return dict(zip(names, res))


def _split3(a):
    hi = a.astype(BF16)
    r1 = a - hi.astype(F32)
    mid = r1.astype(BF16)
    lo = (r1 - mid.astype(F32)).astype(BF16)
    return hi, mid, lo


def _cumsum_body(x_ref, tri_ref, o_ref, *, seq):
    tri = tri_ref[...]
    carry = jnp.zeros((x_ref.shape[0], 1), F32)
    for c in range(seq // LANES):
        hi, mid, lo = _split3(x_ref[:, c * LANES:(c + 1) * LANES])
        cs = (jnp.dot(hi, tri, preferred_element_type=F32) + jnp.dot(mid, tri, preferred_element_type=F32)
              + jnp.dot(lo, tri, preferred_element_type=F32)) + carry
        o_ref[:, c * LANES:(c + 1) * LANES] = cs
        carry = cs[:, LANES - 1:LANES]


def _cumsum_lanes(xt, batch, seq, tri):
    n = xt.shape[0]
    return pl.pallas_call(
        functools.partial(_cumsum_body, seq=seq),
        grid=(batch,),
        in_specs=[pl.BlockSpec((n, seq), lambda b: (0, b)), pl.BlockSpec(tri.shape, lambda b: (0, 0))],
        out_specs=pl.BlockSpec((n, seq), lambda b: (0, b)),
        out_shape=jax.ShapeDtypeStruct(xt.shape, F32),
        compiler_params=_cparams(1),
        name="cumsum",
    )(xt, tri)


def _fox_prompt_body(q_ref, k_ref, v_ref, ft_ref, sz_ref, o_ref, *, tile):
    i = pl.program_id(1)
    rows = lax.broadcasted_iota(I32, (tile, tile), 0)
    cols = lax.broadcasted_iota(I32, (tile, tile), 1)
    for h in range(N_HEADS):
        hs = slice(h * HEAD_DIM, (h + 1) * HEAD_DIM)
        q = q_ref[:, hs]

        def step(c, carry, diagonal, q=q, hs=hs, h=h):
            m, l, acc = carry
            k0 = pl.multiple_of(c * tile, tile)
            s = (lax.dot_general(q, k_ref[pl.ds(k0, tile), hs], NT, preferred_element_type=F32)
                 - ft_ref[h:h + 1, pl.ds(k0, tile)])
            if diagonal:
                s = jnp.where(cols <= rows, s, NEG)
            m_new = jnp.maximum(m, jnp.max(s, axis=-1, keepdims=True))
            p = jnp.exp(s - m_new)
            corr = jnp.exp(m - m_new)
            l = l * corr + jnp.sum(p, axis=-1, keepdims=True)
            acc = acc * corr + jnp.dot(p.astype(BF16), v_ref[pl.ds(k0, tile), hs], preferred_element_type=F32)
            return m_new, l, acc

        init = (jnp.full((tile, 1), NEG, F32), jnp.zeros((tile, 1), F32), jnp.zeros((tile, HEAD_DIM), F32))
        carry = lax.fori_loop(0, i, functools.partial(step, diagonal=False), init)
        _, l, acc = step(i, carry, True)
        o_ref[:, hs] = ((acc / l) * sz_ref[:, hs].astype(F32)).astype(BF16)


def _fox_prompt(pr, ft, batch, seq):
    tile = min(256, seq)
    nq = seq // tile
    rowspec = pl.BlockSpec((tile, HW), lambda b, i: (b * nq + i, 0))
    seqspec = pl.BlockSpec((seq, HW), lambda b, i: (b, 0))
    return pl.pallas_call(
        functools.partial(_fox_prompt_body, tile=tile),
        grid=(batch, nq),
        in_specs=[rowspec, seqspec, seqspec, pl.BlockSpec((N_HEADS, seq), lambda b, i: (0, b)), rowspec],
        out_specs=rowspec,
        out_shape=jax.ShapeDtypeStruct((batch * seq, HW), BF16),
        compiler_params=_cparams(2, VMEM_LIMIT),
        name="fox_prompt",
    )(pr["qb16"], pr["kb16"], pr["vb16"], ft, pr["szb"])


def _count(pred):
    return jnp.sum(jnp.where(pred, 1.0, 0.0), axis=-1, keepdims=True)


def _kth_largest(x_ref, extra, kf, lo0, hi0):
    def cnt_gt(v):
        c = _count(x_ref[...] > v)
        return c if extra is None else c + jnp.where(extra > v, 1.0, 0.0)

    def cnt_ge(v):
        c = _count(x_ref[...] >= v)
        return c if extra is None else c + jnp.where(extra >= v, 1.0, 0.0)

    def max_where(pred_fn):
        x = x_ref[...]
        m = jnp.max(jnp.where(pred_fn(x), x, -jnp.inf), axis=-1, keepdims=True)
        return m if extra is None else jnp.maximum(m, jnp.where(pred_fn(extra), extra, -jnp.inf))

    def bisect(_, c):
        lo, hi = c
        mid = 0.5 * (lo + hi)
        above = cnt_gt(mid) >= kf
        return jnp.where(above, mid, lo), jnp.where(above, hi, mid)

    lo, hi = lax.fori_loop(0, N_BISECT, bisect, (lo0, hi0))

    def finish(c):
        lo, hi, _ = c
        lo, hi = lax.fori_loop(0, 4, bisect, (lo, hi))
        t1 = max_where(lambda x: x <= hi)
        ok = cnt_ge(t1) >= kf
        t2 = max_where(lambda x: x < t1)
        pending = jnp.sum(jnp.where(ok, 0.0, 1.0))
        return jnp.where(ok, t1, lo), jnp.where(ok, t1, t2), pending

    _, hi, _ = lax.while_loop(lambda c: c[2] > 0.0, finish, (lo, hi, jnp.float32(1.0)))
    return hi


def _topk_mask(x_ref, madd_ref, tri_ref, extra, kf, thr):
    x = x_ref[...]
    c_gt = _count(x > thr)
    c_ge = _count(x >= thr)
    if extra is not None:
        c_gt = c_gt + jnp.where(extra > thr, 1.0, 0.0)
        c_ge = c_ge + jnp.where(extra >= thr, 1.0, 0.0)
    need = kf - c_gt
    madd_ref[...] = jnp.where(x >= thr, 0.0, NEG)
    surplus = jnp.sum(jnp.where(c_ge > kf, 1.0, 0.0))

    @pl.when(surplus > 0.0)
    def _():
        tri = tri_ref[...]
        seen = jnp.zeros_like(thr)
        for c in range(x_ref.shape[1] // LANES):
            cs = slice(c * LANES, (c + 1) * LANES)
            xc = x_ref[:, cs]
            tie = jnp.where(xc == thr, 1.0, 0.0)
            rank = jnp.dot(tie.astype(BF16), tri, preferred_element_type=F32) + seen
            madd_ref[:, cs] = jnp.where(xc > thr, 0.0, jnp.where(xc == thr, jnp.where(rank <= need, 0.0, NEG), NEG))
            seen = rank[:, LANES - 1:LANES]

    if extra is None:
        return None
    ties_before = c_ge - c_gt - jnp.where(extra == thr, 1.0, 0.0)
    return jnp.where(extra > thr, 0.0, jnp.where(extra == thr, jnp.where(ties_before + 1.0 <= need, 0.0, NEG), NEG))


def _dsa_prompt_body(rb_ref, q_ref, iq_ref, iw_ref, sz_ref, ik_ref, k_ref, v_ref, tri_ref, o_ref,
                     bias_ref, sc_ref, madd_ref, *, seq, topk, tq):
    b = pl.program_id(0)
    i = pl.program_id(1)

    @pl.when((b == 0) & (i == 0))
    def _bias_tiles():
        ii = lax.broadcasted_iota(I32, (tq, tq), 0)
        jj = lax.broadcasted_iota(I32, (tq, tq), 1)
        for t in range(2):
            bucket = _t5_bucket(ii - jj + tq * t)
            for h in range(N_HEADS):
                acc = jnp.zeros((tq, tq), F32)
                for kb in range(N_BUCKETS):
                    acc = jnp.where(bucket == kb, rb_ref[kb, h], acc)
                bias_ref[h, t] = acc

    pos_q = i * tq + lax.broadcasted_iota(I32, (tq, 1), 0)
    col = lax.broadcasted_iota(I32, (tq, seq), 1)
    adm = col <= pos_q

    sc = jnp.zeros((tq, seq), F32)
    ik = ik_ref[...]
    for h in range(N_HEADS):
        r = lax.dot_general(iq_ref[:, h * D_IDX:(h + 1) * D_IDX], ik, NT, preferred_element_type=F32)
        sc = sc + jnp.maximum(r, 0.0) * iw_ref[:, h:h + 1]
    sc = sc + 0.0
    sc_ref[...] = jnp.where(adm, sc, -jnp.inf)
    lo0 = jnp.min(jnp.where(adm, sc, jnp.inf), axis=-1, keepdims=True)
    hi0 = jnp.max(jnp.where(adm, sc, -jnp.inf), axis=-1, keepdims=True)
    kf = jnp.minimum(topk, pos_q + 1).astype(F32)
    thr = _kth_largest(sc_ref, None, kf, lo0, hi0)
    _topk_mask(sc_ref, madd_ref, tri_ref, None, kf, thr)

    d0 = pl.multiple_of(i * tq, tq)
    d1 = pl.multiple_of(jnp.maximum(i - 1, 0) * tq, tq)
    for h in range(N_HEADS):
        hs = slice(h * HEAD_DIM, (h + 1) * HEAD_DIM)
        far = rb_ref[FAR_BUCKET, h]
        sc_ref[...] = lax.dot_general(q_ref[:, hs], k_ref[:, hs], NT, preferred_element_type=F32) + (madd_ref[...] + far)
        sc_ref[:, pl.ds(d0, tq)] += bias_ref[h, 0] - far

        @pl.when(i > 0)
        def _(h=h, far=far):
            sc_ref[:, pl.ds(d1, tq)] += bias_ref[h, 1] - far

        lg = sc_ref[...]
        p = jnp.exp(lg - jnp.max(lg, axis=-1, keepdims=True))
        den = jnp.sum(p, axis=-1, keepdims=True)
        o = jnp.dot(p.astype(BF16), v_ref[:, hs], preferred_element_type=F32) / den
        o_ref[:, hs] = (o * sz_ref[:, hs].astype(F32)).astype(BF16)


def _dsa_prompt(pr, rel_bias, tri, batch, seq):
    tq = LANES
    assert seq % tq == 0
    nq = seq // tq
    topk = min(TOPK_MAX, seq // 4)
    rowspec = lambda n: pl.BlockSpec((tq, n), lambda b, i: (b * nq + i, 0))
    seqspec = lambda n: pl.BlockSpec((seq, n), lambda b, i: (b, 0))
    return pl.pallas_call(
        functools.partial(_dsa_prompt_body, seq=seq, topk=topk, tq=tq),
        grid=(batch, nq),
        in_specs=[pl.BlockSpec(memory_space=pltpu.SMEM), rowspec(HW), rowspec(HW), rowspec(N_HEADS), rowspec(HW),
                  seqspec(D_IDX), seqspec(HW), seqspec(HW), pl.BlockSpec(tri.shape, lambda b, i: (0, 0))],
        out_specs=rowspec(HW),
        out_shape=jax.ShapeDtypeStruct((batch * seq, HW), BF16),
        scratch_shapes=[pltpu.VMEM((N_HEADS, 2, tq, tq), F32), pltpu.VMEM((tq, seq), F32),
                        pltpu.VMEM((tq, seq), F32)],
        compiler_params=_cparams(2, VMEM_LIMIT),
        name="dsa_prompt",
    )(rel_bias, pr["qa16"], pr["iq16"], pr["iw"], pr["sza"], pr["ik16"], pr["ka16"], pr["va16"], tri)


def _merge_body(x_ref, ua_ref, ub_ref, sga_ref, sgb_ref, wua_ref, wub_ref, wo_ref, y_ref):
    ya = jnp.dot(ua_ref[...], wua_ref[...], preferred_element_type=F32)
    yb = jnp.dot(ub_ref[...], wub_ref[...], preferred_element_type=F32)
    m = sga_ref[...].astype(F32) * ya + sgb_ref[...].astype(F32) * yb
    y_ref[...] = x_ref[...] + jnp.dot(m.astype(BF16), wo_ref[...], preferred_element_type=F32)


def _merge(x2d, ua, ub, sga, sgb, wts):
    rows, d_model = x2d.shape
    tm = min(512, rows)
    assert rows % tm == 0
    row = lambda n: pl.BlockSpec((tm, n), lambda i: (i, 0))
    full = lambda a: pl.BlockSpec(a.shape, lambda i: (0, 0))
    return pl.pallas_call(
        _merge_body,
        grid=(rows // tm,),
        in_specs=[row(d_model), row(HW), row(HW), row(d_model), row(d_model),
                  full(wts["wua"]), full(wts["wub"]), full(wts["wo"])],
        out_specs=row(d_model),
        out_shape=jax.ShapeDtypeStruct((rows, d_model), F32),
        compiler_params=_cparams(1, VMEM_LIMIT),
        name="merge",
    )(x2d, ua, ub, sga, sgb, wts["wua"], wts["wub"], wts["wo"])


def _idx_sample_body(pt_ref, iq_ref, iw_ref, *refs):
    pages, o_ref = refs[:PAGES_PER_STEP], refs[PAGES_PER_STEP]
    iq = iq_ref[...]
    iw = iw_ref[...]
    rows = []
    for pg in pages:
        r = lax.dot_general(iq, pg[...].astype(BF16), NT, preferred_element_type=F32)
        rows.append(jnp.sum(jnp.maximum(r, 0.0) * iw, axis=0, keepdims=True))
    o_ref[...] = jnp.concatenate(rows, axis=0)


def _idx_sample(iq3, iw3, cache_ik, page_table):
    bd, n_pages = page_table.shape
    assert n_pages % PAGES_PER_STEP == 0
    ng = n_pages // PAGES_PER_STEP
    page_specs = [pl.BlockSpec((None, PAGE, D_IDX), lambda b, g, pt, i=i: (pt[b, g * PAGES_PER_STEP + i], 0, 0))
                  for i in range(PAGES_PER_STEP)]
    return pl.pallas_call(
        _idx_sample_body,
        grid_spec=pltpu.PrefetchScalarGridSpec(
            num_scalar_prefetch=1, grid=(bd, ng),
            in_specs=[pl.BlockSpec((None, QPAD, D_IDX), lambda b, g, pt: (b, 0, 0)),
                      pl.BlockSpec((None, QPAD, 1), lambda b, g, pt: (b, 0, 0))] + page_specs,
            out_specs=pl.BlockSpec((None, PAGES_PER_STEP, PAGE), lambda b, g, pt: (b, g, 0))),
        out_shape=jax.ShapeDtypeStruct((bd, n_pages, PAGE), F32),
        compiler_params=_cparams(2),
        name="idx_sample",
    )(page_table, iq3, iw3, *([cache_ik] * PAGES_PER_STEP))


def _select_sample_body(sc_ref, iq_ref, ik_ref, iw_ref, tri_ref, madd_ref, maddn_ref, *, topk):
    bd = sc_ref.shape[0]
    ii = lax.broadcasted_iota(I32, (bd, bd), 0)
    jj = lax.broadcasted_iota(I32, (bd, bd), 1)
    ik = ik_ref[...]
    snew = jnp.zeros((bd, 1), F32)
    for h in range(N_HEADS):
        r = lax.dot_general(iq_ref[:, h * D_IDX:(h + 1) * D_IDX], ik, NT, preferred_element_type=F32)
        rd = jnp.sum(jnp.where(ii == jj, r, 0.0), axis=-1, keepdims=True)
        snew = snew + jnp.maximum(rd, 0.0) * iw_ref[:, h:h + 1]
    snew = snew + 0.0
    x = sc_ref[...]
    kf = jnp.full((bd, 1), float(topk), F32)
    lo0 = jnp.minimum(jnp.min(x, axis=-1, keepdims=True), snew)
    hi0 = jnp.maximum(jnp.max(x, axis=-1, keepdims=True), snew)
    thr = _kth_largest(sc_ref, snew, kf, lo0, hi0)
    mn = _topk_mask(sc_ref, madd_ref, tri_ref, snew, kf, thr)
    maddn_ref[...] = jnp.broadcast_to(mn, maddn_ref.shape)


def _select_sample(scores2d, iq16, ik16, iw, tri, n_new):
    bd, past = scores2d.shape
    topk = min(TOPK_MAX, (past + n_new) // 4)
    full = lambda a: pl.BlockSpec(a.shape, lambda i: (0,) * a.ndim)
    return pl.pallas_call(
        functools.partial(_select_sample_body, topk=topk),
        grid=(1,),
        in_specs=[full(scores2d), full(iq16), full(ik16), full(iw), full(tri)],
        out_specs=[pl.BlockSpec((bd, past), lambda i: (0, 0)), pl.BlockSpec((bd, LANES), lambda i: (0, 0))],
        out_shape=[jax.ShapeDtypeStruct((bd, past), F32), jax.ShapeDtypeStruct((bd, LANES), F32)],
        compiler_params=_cparams(1, VMEM_LIMIT),
        name="select_sample",
    )(scores2d, iq16, ik16, iw, tri)


def _head_mask():
    r = lax.broadcasted_iota(I32, (QPAD, HW), 0)
    c = lax.broadcasted_iota(I32, (QPAD, HW), 1)
    return jnp.where((c >= r * HEAD_DIM) & (c < (r + 1) * HEAD_DIM), 1.0, 0.0)


def _online_update(m_ref, l_ref, acc_ref, logits, v_pages):
    m_old = m_ref[...]
    m_new = jnp.maximum(m_old, jnp.max(logits, axis=-1, keepdims=True))
    p = jnp.exp(logits - m_new)
    corr = jnp.exp(m_old - m_new)
    l_ref[...] = l_ref[...] * corr + jnp.sum(p, axis=-1, keepdims=True)
    pv = jnp.zeros(acc_ref.shape, F32)
    for i, vp in enumerate(v_pages):
        pv = pv + jnp.dot(p[:, i * PAGE:(i + 1) * PAGE].astype(BF16), vp[...].astype(BF16), preferred_element_type=F32)
    acc_ref[...] = acc_ref[...] * corr + pv
    m_ref[...] = m_new


def _finish_new_token(m_ref, l_ref, acc_ref, logit_new, vnew_ref, hmask, sz_ref, o_ref):
    m_old = m_ref[...]
    m_new = jnp.maximum(m_old, logit_new)
    p_new = jnp.exp(logit_new - m_new)
    corr = jnp.exp(m_old - m_new)
    den = l_ref[...] * corr + p_new
    o = (acc_ref[...] * corr + p_new * vnew_ref[...]) / den
    row = jnp.sum(o * hmask, axis=0, keepdims=True)
    o_ref[...] = (row * sz_ref[...].astype(F32)).astype(BF16)


def _dsa_sample_body(pt_ref, rbt_ref, q_ref, knew_ref, vnew_ref, sz_ref, madd_ref, maddn_ref, *refs):
    n = PAGES_PER_STEP
    k_pages, v_pages = refs[:n], refs[n:2 * n]
    o_ref, m_ref, l_ref, acc_ref, blast_ref = refs[2 * n:]
    b = pl.program_id(0)
    g = pl.program_id(1)
    last = g == pl.num_programs(1) - 1
    hmask = _head_mask()
    qbd = (q_ref[...].astype(F32) * hmask).astype(BF16)

    @pl.when((b == 0) & (g == 0))
    def _last_page_bias():
        bucket = _t5_bucket(PAGE - lax.broadcasted_iota(I32, (QPAD, PAGE), 1))
        acc = jnp.zeros((QPAD, PAGE), F32)
        for kb in range(N_BUCKETS):
            acc = jnp.where(bucket == kb, rbt_ref[:, kb:kb + 1], acc)
        blast_ref[...] = acc

    @pl.when(g == 0)
    def _init():
        m_ref[...] = jnp.full(m_ref.shape, NEG, F32)
        l_ref[...] = jnp.zeros(l_ref.shape, F32)
        acc_ref[...] = jnp.zeros(acc_ref.shape, F32)

    far = rbt_ref[:, FAR_BUCKET:FAR_BUCKET + 1]
    parts = []
    for i in range(n):
        lt = lax.dot_general(qbd, k_pages[i][...].astype(BF16), NT, preferred_element_type=F32)
        bias = far if i < n - 1 else jnp.where(last, blast_ref[...], jnp.broadcast_to(far, (QPAD, PAGE)))
        parts.append(lt + madd_ref[i:i + 1, :] + bias)
    _online_update(m_ref, l_ref, acc_ref, jnp.concatenate(parts, axis=1), v_pages)

    @pl.when(last)
    def _finish():
        ln = jnp.sum(qbd.astype(F32) * knew_ref[...], axis=-1, keepdims=True)
        ln = ln + rbt_ref[:, 0:1] + maddn_ref[:, 0:1]
        _finish_new_token(m_ref, l_ref, acc_ref, ln, vnew_ref, hmask, sz_ref, o_ref)


def _fox_sample_body(pt_ref, q_ref, knew_ref, vnew_ref, sz_ref, fn_ref, *refs):
    n = PAGES_PER_STEP
    k_pages, v_pages, lf_pages = refs[:n], refs[n:2 * n], refs[2 * n:3 * n]
    o_ref, m_ref, l_ref, acc_ref, c_ref = refs[3 * n:]
    g = pl.program_id(1)
    last = g == pl.num_programs(1) - 1
    hmask = _head_mask()
    qbd = (q_ref[...].astype(F32) * hmask).astype(BF16)

    @pl.when(g == 0)
    def _init():
        m_ref[...] = jnp.full(m_ref.shape, NEG, F32)
        l_ref[...] = jnp.zeros(l_ref.shape, F32)
        acc_ref[...] = jnp.zeros(acc_ref.shape, F32)
        c_ref[...] = jnp.zeros(c_ref.shape, F32)

    lf = jnp.concatenate([pg[...] for pg in lf_pages], axis=0)
    lane = lax.broadcasted_iota(I32, lf.shape, 1)
    suf = lf
    k = 1
    while k < PAGE:
        suf = suf + jnp.where(lane + k < PAGE, pltpu.roll(suf, PAGE - k, axis=1), 0.0)
        k *= 2
    later = c_ref[...]
    zpad = jnp.zeros((QPAD - N_HEADS, PAGE), F32)
    parts = []
    for i in range(n):
        rs = slice(i * N_HEADS, (i + 1) * N_HEADS)
        lt = lax.dot_general(qbd, k_pages[i][...].astype(BF16), NT, preferred_element_type=F32)
        bias = (suf[rs] - lf[rs]) + (later + fn_ref[...])[:N_HEADS]
        parts.append(lt + jnp.concatenate([bias, zpad], axis=0))
        later = later + jnp.concatenate([suf[rs, 0:1], zpad[:, 0:1]], axis=0)
    c_ref[...] = later
    _online_update(m_ref, l_ref, acc_ref, jnp.concatenate(parts, axis=1), v_pages)

    @pl.when(last)
    def _finish():
        ln = jnp.sum(qbd.astype(F32) * knew_ref[...], axis=-1, keepdims=True)
        _finish_new_token(m_ref, l_ref, acc_ref, ln, vnew_ref, hmask, sz_ref, o_ref)


def _row3(a):
    return a.reshape(a.shape[0], 1, a.shape[1])


def _sample_attention(body, name, page_table, small_inputs, small_specs, paged_inputs, paged_specs, bd, ng,
                      extra_scratch):
    rowspec = pl.BlockSpec((None, 1, HW), lambda b, g, pt: (b, 0, 0))
    out = pl.pallas_call(
        body,
        grid_spec=pltpu.PrefetchScalarGridSpec(
            num_scalar_prefetch=1, grid=(bd, ng),
            in_specs=small_specs + paged_specs,
            out_specs=rowspec,
            scratch_shapes=[pltpu.VMEM((QPAD, 1), F32), pltpu.VMEM((QPAD, 1), F32), pltpu.VMEM((QPAD, HW), F32),
                            extra_scratch]),
        out_shape=jax.ShapeDtypeStruct((bd, 1, HW), BF16),
        compiler_params=_cparams(2, VMEM_LIMIT),
        name=name,
    )(page_table, *small_inputs, *paged_inputs)
    return out.reshape(bd, HW)


def _dsa_sample(sp, rbt, madd3, maddn, cache_k, cache_v, page_table):
    bd, n_pages = page_table.shape
    n = PAGES_PER_STEP
    ng = n_pages // n
    rowspec = pl.BlockSpec((None, 1, HW), lambda b, g, pt: (b, 0, 0))
    page = lambda i: pl.BlockSpec((None, PAGE, HW), lambda b, g, pt, i=i: (pt[b, g * n + i], 0, 0))
    small_inputs = [rbt, _row3(sp["qa16"]), _row3(sp["ka"]), _row3(sp["va"]), _row3(sp["sza"]), madd3, _row3(maddn)]
    small_specs = [pl.BlockSpec(rbt.shape, lambda b, g, pt: (0, 0)), rowspec, rowspec, rowspec, rowspec,
                   pl.BlockSpec((None, n, PAGE), lambda b, g, pt: (b, g, 0)),
                   pl.BlockSpec((None, 1, LANES), lambda b, g, pt: (b, 0, 0))]
    return _sample_attention(_dsa_sample_body, "dsa_sample", page_table, small_inputs, small_specs,
                             [cache_k] * n + [cache_v] * n, [page(i) for i in range(n)] * 2, bd, ng,
                             pltpu.VMEM((QPAD, PAGE), F32))


def _fox_sample(sp, fn3, cache_k, cache_v, cache_lft, page_table):
    bd, n_pages = page_table.shape
    n = PAGES_PER_STEP
    ng = n_pages // n
    rowspec = pl.BlockSpec((None, 1, HW), lambda b, g, pt: (b, 0, 0))
    rev = lambda b, g, pt, i: pt[b, n_pages - 1 - (g * n + i)]
    page = lambda i: pl.BlockSpec((None, PAGE, HW), lambda b, g, pt, i=i: (rev(b, g, pt, i), 0, 0))
    lfpage = lambda i: pl.BlockSpec((None, N_HEADS, PAGE), lambda b, g, pt, i=i: (rev(b, g, pt, i), 0, 0))
    small_inputs = [_row3(sp["qb16"]), _row3(sp["kb"]), _row3(sp["vb"]), _row3(sp["szb"]), fn3]
    small_specs = [rowspec, rowspec, rowspec, rowspec, pl.BlockSpec((None, QPAD, 1), lambda b, g, pt: (b, 0, 0))]
    return _sample_attention(_fox_sample_body, "fox_sample", page_table, small_inputs, small_specs,
                             [cache_k] * n + [cache_v] * n + [cache_lft] * n,
                             [page(i) for i in range(n)] * 2 + [lfpage(i) for i in range(n)], bd, ng,
                             pltpu.VMEM((QPAD, 1), F32))


def _prepare_weights(d_model, g_norm, w_in, b_fgate, qn_a, kn_a, qn_b, kn_b, w_up_a, w_up_b, w_out):
    widths = (HW, HW, HW, HW, N_HEADS * D_IDX, N_HEADS, D_IDX, HW, HW, HW, HW, N_HEADS, d_model, d_model)
    offs = [0]
    for w in widths:
        offs.append(offs[-1] + w)
    col = lambda a, b: w_in[:, offs[a]:offs[b]]
    small = jnp.concatenate([col(6, 7), col(5, 6), col(11, 12),
                             jnp.zeros((d_model, LANES - D_IDX - 2 * N_HEADS), w_in.dtype)], axis=1)
    tile = lambda v: jnp.tile(v.astype(F32), N_HEADS)
    blk = jnp.arange(HW) // HEAD_DIM
    return {
        "g": g_norm.astype(F32).reshape(1, d_model),
        "wa": col(0, 5).astype(BF16), "ws": small.astype(BF16), "wb": col(7, 11).astype(BF16),
        "wg": col(12, 14).astype(BF16), "wft": col(11, 12).T.astype(BF16),
        "bf": b_fgate.astype(F32).reshape(1, N_HEADS), "bft": b_fgate.astype(F32).reshape(N_HEADS, 1),
        "nrm": jnp.stack([tile(qn_a), tile(kn_a), tile(qn_b), tile(kn_b)]),
        "gm": jnp.where(blk[:, None] == blk[None, :], 1.0 / HEAD_DIM, 0.0).astype(BF16),
        "wua": w_up_a.astype(BF16), "wub": w_up_b.astype(BF16), "wo": w_out.astype(BF16),
    }


def _pad_rows(a, rows):
    return jnp.pad(a, ((0, 0), (0, rows - a.shape[1])) + ((0, 0),) * (a.ndim - 2))


def kernel(x_prompt, x_sample, cache_a_k, cache_a_v, cache_a_idx_k, cache_b_k, cache_b_v, cache_b_logf, page_table,
           rel_bias, g_norm, w_in, b_fgate, qnorm_a, knorm_a, qnorm_b, knorm_b, w_up_a, w_up_b, w_out):
    batch, seq, d_model = x_prompt.shape
    bd, t_new, _ = x_sample.shape
    depth, n_pool = cache_a_k.shape[:2]
    n_pages = page_table.shape[1]
    past = n_pages * PAGE
    assert t_new == 1, "the decode kernels handle one new token per sequence"
    assert cache_a_k.shape[2:] == (PAGE, N_HEADS, HEAD_DIM)

    rel_bias = rel_bias.astype(F32)
    rbt = jnp.pad(rel_bias.T, ((0, QPAD - N_HEADS), (0, 0)))
    tri = jnp.triu(jnp.ones((LANES, LANES), F32)).astype(BF16)

    xp = x_prompt.reshape(batch * seq, d_model)
    xs = x_sample.reshape(bd * t_new, d_model)
    st_p, st_s = [], []
    for layer in range(depth):
        wts = _prepare_weights(d_model, g_norm[layer], w_in[layer], b_fgate[layer], qnorm_a[layer], knorm_a[layer],
                               qnorm_b[layer], knorm_b[layer], w_up_a[layer], w_up_b[layer], w_out[layer])
        pr = _project(xp, wts)
        ft = _cumsum_lanes(pr["lft"], batch, seq, tri)
        ub = _fox_prompt(pr, ft, batch, seq)
        ua = _dsa_prompt(pr, rel_bias, tri, batch, seq)
        st_p.append((pr["ka"].reshape(batch, seq, N_HEADS, HEAD_DIM), pr["va"].reshape(batch, seq, N_HEADS, HEAD_DIM),
                     pr["ik"].reshape(batch, seq, D_IDX),
                     pr["kb"].reshape(batch, seq, N_HEADS, HEAD_DIM), pr["vb"].reshape(batch, seq, N_HEADS, HEAD_DIM),
                     pr["lf"].reshape(batch, seq, N_HEADS)))
        xp = _merge(xp, ua, ub, pr["sga"], pr["sgb"], wts)

        sp = _project(xs, wts)
        iq3 = _pad_rows(sp["iq16"].reshape(bd, N_HEADS, D_IDX), QPAD)
        iw3 = _pad_rows(sp["iw"].reshape(bd, N_HEADS, 1), QPAD)
        scores = _idx_sample(iq3, iw3, cache_a_idx_k[layer], page_table)
        madd, maddn = _select_sample(scores.reshape(bd, past), sp["iq16"], sp["ik16"], sp["iw"], tri, t_new)
        ua_s = _dsa_sample(sp, rbt, madd.reshape(bd, n_pages, PAGE), maddn,
                           cache_a_k[layer].reshape(n_pool, PAGE, HW), cache_a_v[layer].reshape(n_pool, PAGE, HW),
                           page_table)
        fn3 = _pad_rows(sp["lf"].reshape(bd, N_HEADS, 1), QPAD)
        ub_s = _fox_sample(sp, fn3, cache_b_k[layer].reshape(n_pool, PAGE, HW),
                           cache_b_v[layer].reshape(n_pool, PAGE, HW),
                           jnp.swapaxes(cache_b_logf[layer], 1, 2), page_table)
        st_s.append((sp["ka"].reshape(bd, t_new, N_HEADS, HEAD_DIM), sp["va"].reshape(bd, t_new, N_HEADS, HEAD_DIM),
                     sp["ik"].reshape(bd, t_new, D_IDX),
                     sp["kb"].reshape(bd, t_new, N_HEADS, HEAD_DIM), sp["vb"].reshape(bd, t_new, N_HEADS, HEAD_DIM),
                     sp["lf"].reshape(bd, t_new, N_HEADS)))
        xs = _merge(xs, ua_s, ub_s, sp["sga"], sp["sgb"], wts)

    outs_p = [jnp.stack(z) for z in zip(*st_p)]
    outs_s = [jnp.stack(z) for z in zip(*st_s)]
    return (xp.reshape(batch, seq, d_model), xs.reshape(bd, t_new, d_model), *outs_p, *outs_s)
```

```python
import functools
import math

import jax
import jax.numpy as jnp
from jax import lax
from jax.experimental import pallas as pl
from jax.experimental.pallas import tpu as pltpu

F32 = jnp.float32
BF16 = jnp.bfloat16
I32 = jnp.int32

HEAD_DIM = 64
N_HEADS = 8
HW = N_HEADS * HEAD_DIM
D_IDX = 64
PAGE = 128
TOPK_MAX = 256
N_BUCKETS = 32
MAX_EXACT = N_BUCKETS // 2
MAX_DISTANCE = 128
FAR_BUCKET = N_BUCKETS - 1
EPS = 1e-6
NEG = -1e30
SCALE = HEAD_DIM ** -0.5
IDX_W_SCALE = (N_HEADS * D_IDX) ** -0.5

LANES = 128
ATTN_PAGES = 16
IDX_PAGES = 32
QPAD = 16
N_BISECT = 22
CAUSAL_STEP = 512
VMEM_LIMIT = 56 * 1024 * 1024

NT = (((1,), (1,)), ((), ()))


def _cparams(n_axes, vmem=None):
    return pltpu.CompilerParams(dimension_semantics=("arbitrary",) * n_axes, vmem_limit_bytes=vmem)


def _log_sigmoid(z):
    return -(jnp.maximum(-z, 0.0) + jnp.log1p(jnp.exp(-jnp.abs(z))))


def _t5_bucket(dist):
    d = jnp.maximum(dist, 0)
    df = jnp.maximum(d, 1).astype(F32)
    large = MAX_EXACT + (jnp.log(df / MAX_EXACT) / math.log(MAX_DISTANCE / MAX_EXACT)
                         * (N_BUCKETS - MAX_EXACT)).astype(I32)
    large = jnp.minimum(large, N_BUCKETS - 1)
    return jnp.where(d < MAX_EXACT, d, large)


def _head_rows(h):
    if isinstance(h, int):
        return slice(h * HEAD_DIM, (h + 1) * HEAD_DIM)
    return pl.ds(pl.multiple_of(h * HEAD_DIM, HEAD_DIM), HEAD_DIM)


def _causal_widths(seq):
    step = CAUSAL_STEP if seq % CAUSAL_STEP == 0 else seq
    return step, [step * (t + 1) for t in range(seq // step)]


def _project_body(x_ref, g_ref, wr_ref, wt_ref, bft_ref, nrm_ref, nrmt_ref, gm_ref,
                  qa_o, sza_o, iq_o, iw_o, qb_o, szb_o, sga_o, sgb_o,
                  kat_o, vat_o, kbt_o, vbt_o, kat16_o, vat16_o, kbt16_o, vbt16_o, ikt_o, ikt16_o, lft_o,
                  *, d_model, tm):
    x = x_ref[...]
    ms = jnp.mean(x * x, axis=-1, keepdims=True)
    h = ((x * lax.rsqrt(ms + EPS)) * g_ref[...]).astype(BF16)
    gm = gm_ref[...]
    nd = d_model // HW

    def rows(i, n=HW):
        return lax.dot_general(h, wr_ref[i * HW:i * HW + n, :], NT, preferred_element_type=F32)

    def cols(i, n=HW):
        return lax.dot_general(wt_ref[i * HW:i * HW + n, :], h, NT, preferred_element_type=F32)

    def head_norm(p, row):
        ss = jnp.dot((p * p).astype(BF16), gm, preferred_element_type=F32)
        return (p * lax.rsqrt(ss + EPS)) * nrm_ref[row:row + 1, :]

    def head_norm_t(pt, j):
        ss = jnp.dot(gm, (pt * pt).astype(BF16), preferred_element_type=F32)
        return (pt * lax.rsqrt(ss + EPS)) * jnp.concatenate([nrmt_ref[j]] * (tm // LANES), axis=1)

    def silu(z):
        return z * jax.nn.sigmoid(z)

    qa_o[...] = (head_norm(rows(0), 0) * SCALE).astype(BF16)
    sza_o[...] = silu(rows(1)).astype(BF16)
    iq_o[...] = rows(2).astype(BF16)
    qb_o[...] = (head_norm(rows(3), 1) * SCALE).astype(BF16)
    szb_o[...] = silu(rows(4)).astype(BF16)
    for c in range(nd):
        sga_o[:, c * HW:(c + 1) * HW] = jax.nn.sigmoid(rows(5 + c)).astype(BF16)
        sgb_o[:, c * HW:(c + 1) * HW] = jax.nn.sigmoid(rows(5 + nd + c)).astype(BF16)
    iw_o[...] = rows(5 + 2 * nd, LANES)[:, 0:N_HEADS] * IDX_W_SCALE

    kat = head_norm_t(cols(0), 0)
    kat_o[...] = kat
    kat16_o[...] = kat.astype(BF16)
    vat = cols(1)
    vat_o[...] = vat
    vat16_o[...] = vat.astype(BF16)
    kbt = head_norm_t(cols(2), 1)
    kbt_o[...] = kbt
    kbt16_o[...] = kbt.astype(BF16)
    vbt = cols(3)
    vbt_o[...] = vbt
    vbt16_o[...] = vbt.astype(BF16)
    small = cols(4, LANES)
    ikt = small[0:D_IDX]
    ikt_o[...] = ikt
    ikt16_o[...] = ikt.astype(BF16)
    lft_o[...] = _log_sigmoid(small[D_IDX:D_IDX + N_HEADS] + bft_ref[...])


def _project(x2d, wts, batch, seq):
    rows, d_model = x2d.shape
    assert rows == batch * seq
    tm = min(256, seq)
    assert seq % tm == 0 and tm % LANES == 0 and d_model % HW == 0
    nt = seq // tm
    consts = (wts["g"], wts["wr"], wts["wt"], wts["bft"], wts["nrm"], wts["nrmt"], wts["gm"])

    def row(n, dt):
        return jax.ShapeDtypeStruct((rows, n), dt), pl.BlockSpec((tm, n), lambda b, i: (b * nt + i, 0))

    def col(n, dt):
        return (jax.ShapeDtypeStruct((batch, n, seq), dt), pl.BlockSpec((None, n, tm), lambda b, i: (b, 0, i)))

    outs = [row(HW, BF16), row(HW, BF16), row(HW, BF16), row(N_HEADS, F32), row(HW, BF16), row(HW, BF16),
            row(d_model, BF16), row(d_model, BF16),
            col(HW, F32), col(HW, F32), col(HW, F32), col(HW, F32),
            col(HW, BF16), col(HW, BF16), col(HW, BF16), col(HW, BF16),
            col(D_IDX, F32), col(D_IDX, BF16), col(N_HEADS, F32)]
    names = ("qa16", "sza", "iq16", "iw", "qb16", "szb", "sga", "sgb",
             "kat", "vat", "kbt", "vbt", "kat16", "vat16", "kbt16", "vbt16", "ikt", "ikt16", "lft")
    res = pl.pallas_call(
        functools.partial(_project_body, d_model=d_model, tm=tm),
        grid=(batch, nt),
        in_specs=[pl.BlockSpec((tm, d_model), lambda b, i: (b * nt + i, 0))]
        + [pl.BlockSpec(c.shape, lambda b, i, n=c.ndim: (0,) * n) for c in consts],
        out_specs=[o[1] for o in outs],
        out_shape=[o[0] for o in outs],
        compiler_params=_cparams(2, VMEM_LIMIT),
        name="project",
    )(x2d, *consts)
    return dict(zip(names, res))


def _split3(a):
    hi = a.astype(BF16)
    r1 = a - hi.astype(F32)
    mid = r1.astype(BF16)
    lo = (r1 - mid.astype(F32)).astype(BF16)
    return hi, mid, lo


def _cumsum_body(x_ref, tri_ref, o_ref, *, seq):
    tri = tri_ref[...]
    carry = jnp.zeros((x_ref.shape[0], 1), F32)
    for c in range(seq // LANES):
        hi, mid, lo = _split3(x_ref[:, c * LANES:(c + 1) * LANES])
        cs = (jnp.dot(hi, tri, preferred_element_type=F32) + jnp.dot(mid, tri, preferred_element_type=F32)
              + jnp.dot(lo, tri, preferred_element_type=F32)) + carry
        o_ref[:, c * LANES:(c + 1) * LANES] = cs
        carry = cs[:, LANES - 1:LANES]


def _cumsum_lanes(xt, tri):
    batch, n, seq = xt.shape
    spec = pl.BlockSpec((None, n, seq), lambda b: (b, 0, 0))
    return pl.pallas_call(
        functools.partial(_cumsum_body, seq=seq),
        grid=(batch,),
        in_specs=[spec, pl.BlockSpec(tri.shape, lambda b: (0, 0))],
        out_specs=spec,
        out_shape=jax.ShapeDtypeStruct(xt.shape, F32),
        compiler_params=_cparams(1),
        name="cumsum",
    )(xt, tri)


def _fox_prompt_body(q_ref, kt_ref, vt_ref, ft_ref, sz_ref, o_ref, q8_ref, o8_ref, *, tq, step, widths):
    i = pl.program_id(1)
    for h in range(N_HEADS):
        q8_ref[h] = q_ref[:, _head_rows(h)]
    tier = ((i + 1) * tq + step - 1) // step - 1

    for t, w in enumerate(widths):
        @pl.when(tier == t)
        def _(w=w):
            row_pos = i * tq + lax.broadcasted_iota(I32, (tq, step), 0)
            col_pos = (w - step) + lax.broadcasted_iota(I32, (tq, step), 1)
            visible = col_pos <= row_pos

            def head(h):
                hr = _head_rows(h)
                s = jnp.dot(q8_ref[h], kt_ref[hr, :w], preferred_element_type=F32) - ft_ref[pl.ds(h, 1), :w]
                tail = jnp.where(visible, s[:, w - step:], NEG)
                s = tail if w == step else jnp.concatenate([s[:, :w - step], tail], axis=1)
                p = jnp.exp(s - jnp.max(s, axis=-1, keepdims=True))
                den = jnp.sum(p, axis=-1, keepdims=True)
                o8_ref[h] = lax.dot_general(p.astype(BF16), vt_ref[hr, :w], NT, preferred_element_type=F32) / den

            def pair(j, carry):
                head(2 * j)
                head(2 * j + 1)
                return carry

            lax.fori_loop(0, N_HEADS // 2, pair, 0)

    for h in range(N_HEADS):
        hs = _head_rows(h)
        o_ref[:, hs] = (o8_ref[h] * sz_ref[:, hs].astype(F32)).astype(BF16)


def _fox_prompt(pr, ft, batch, seq):
    tq = min(256, seq)
    nq = seq // tq
    step, widths = _causal_widths(seq)
    assert step % tq == 0
    rowspec = pl.BlockSpec((tq, HW), lambda b, i: (b * nq + i, 0))
    seqspec = lambda n: pl.BlockSpec((None, n, seq), lambda b, i: (b, 0, 0))
    return pl.pallas_call(
        functools.partial(_fox_prompt_body, tq=tq, step=step, widths=widths),
        grid=(batch, nq),
        in_specs=[rowspec, seqspec(HW), seqspec(HW), seqspec(N_HEADS), rowspec],
        out_specs=rowspec,
        out_shape=jax.ShapeDtypeStruct((batch * seq, HW), BF16),
        scratch_shapes=[pltpu.VMEM((N_HEADS, tq, HEAD_DIM), BF16), pltpu.VMEM((N_HEADS, tq, HEAD_DIM), F32)],
        compiler_params=_cparams(2, VMEM_LIMIT),
        name="fox_prompt",
    )(pr["qb16"], pr["kbt16"], pr["vbt16"], ft, pr["szb"])


def _count(pred):
    return jnp.sum(jnp.where(pred, 1.0, 0.0), axis=-1, keepdims=True)


def _kth_largest(x_ref, extra, kf, lo0, hi0):
    def cnt_gt(v):
        c = _count(x_ref[...] > v)
        return c if extra is None else c + jnp.where(extra > v, 1.0, 0.0)

    def cnt_ge(v):
        c = _count(x_ref[...] >= v)
        return c if extra is None else c + jnp.where(extra >= v, 1.0, 0.0)

    def max_where(pred_fn):
        x = x_ref[...]
        m = jnp.max(jnp.where(pred_fn(x), x, -jnp.inf), axis=-1, keepdims=True)
        return m if extra is None else jnp.maximum(m, jnp.where(pred_fn(extra), extra, -jnp.inf))

    def bisect(_, c):
        lo, hi = c
        mid = 0.5 * (lo + hi)
        above = cnt_gt(mid) >= kf
        return jnp.where(above, mid, lo), jnp.where(above, hi, mid)

    lo, hi = lax.fori_loop(0, N_BISECT, bisect, (lo0, hi0))

    def finish(c):
        lo, hi, _ = c
        lo, hi = lax.fori_loop(0, 4, bisect, (lo, hi))
        t1 = max_where(lambda x: x <= hi)
        ok = cnt_ge(t1) >= kf
        t2 = max_where(lambda x: x < t1)
        pending = jnp.sum(jnp.where(ok, 0.0, 1.0))
        return jnp.where(ok, t1, lo), jnp.where(ok, t1, t2), pending

    _, hi, _ = lax.while_loop(lambda c: c[2] > 0.0, finish, (lo, hi, jnp.float32(1.0)))
    return hi


def _topk_mask(x_ref, madd_ref, tri_ref, extra, kf, thr):
    x = x_ref[...]
    c_gt = _count(x > thr)
    c_ge = _count(x >= thr)
    if extra is not None:
        c_gt = c_gt + jnp.where(extra > thr, 1.0, 0.0)
        c_ge = c_ge + jnp.where(extra >= thr, 1.0, 0.0)
    need = kf - c_gt
    madd_ref[...] = jnp.where(x >= thr, 0.0, NEG)
    surplus = jnp.sum(jnp.where(c_ge > kf, 1.0, 0.0))

    @pl.when(surplus > 0.0)
    def _():
        tri = tri_ref[...]
        seen = jnp.zeros_like(thr)
        for c in range(x_ref.shape[1] // LANES):
            cs = slice(c * LANES, (c + 1) * LANES)
            xc = x_ref[:, cs]
            tie = jnp.where(xc == thr, 1.0, 0.0)
            rank = jnp.dot(tie.astype(BF16), tri, preferred_element_type=F32) + seen
            madd_ref[:, cs] = jnp.where(xc > thr, 0.0, jnp.where(xc == thr, jnp.where(rank <= need, 0.0, NEG), NEG))
            seen = rank[:, LANES - 1:LANES]

    if extra is None:
        return None
    ties_before = c_ge - c_gt - jnp.where(extra == thr, 1.0, 0.0)
    return jnp.where(extra > thr, 0.0, jnp.where(extra == thr, jnp.where(ties_before + 1.0 <= need, 0.0, NEG), NEG))


def _dsa_prompt_body(rb_ref, q_ref, iq_ref, iw_ref, sz_ref, ikt_ref, kt_ref, vt_ref, tri_ref, o_ref,
                     bias_ref, sc_ref, madd_ref, q8_ref, o8_ref, *, topk, tq, step, widths):
    b = pl.program_id(0)
    i = pl.program_id(1)

    @pl.when((b == 0) & (i == 0))
    def _bias_tiles():
        ii = lax.broadcasted_iota(I32, (tq, tq), 0)
        jj = lax.broadcasted_iota(I32, (tq, tq), 1)
        for t in range(2):
            bucket = _t5_bucket(ii - jj + tq * t)
            for h in range(N_HEADS):
                acc = jnp.zeros((tq, tq), F32)
                for kb in range(N_BUCKETS):
                    acc = jnp.where(bucket == kb, rb_ref[kb, h], acc)
                bias_ref[h, t] = acc

    for h in range(N_HEADS):
        q8_ref[h] = q_ref[:, _head_rows(h)]
    pos_q = i * tq + lax.broadcasted_iota(I32, (tq, 1), 0)
    kf = jnp.minimum(topk, pos_q + 1).astype(F32)
    d0 = pl.multiple_of(i * tq, tq)
    d1 = pl.multiple_of(jnp.maximum(i - 1, 0) * tq, tq)
    tier = ((i + 1) * tq + step - 1) // step - 1

    for t, w in enumerate(widths):
        @pl.when(tier == t)
        def _(w=w):
            x_ref = sc_ref.at[:, pl.ds(0, w)]
            m_ref = madd_ref.at[:, pl.ds(0, w)]
            adm = lax.broadcasted_iota(I32, (tq, w), 1) <= pos_q
            sc = jnp.zeros((tq, w), F32)
            ikt = ikt_ref[:, :w]
            for h in range(N_HEADS):
                r = jnp.dot(iq_ref[:, h * D_IDX:(h + 1) * D_IDX], ikt, preferred_element_type=F32)
                sc = sc + jnp.maximum(r, 0.0) * iw_ref[:, h:h + 1]
            sc = sc + 0.0
            x_ref[...] = jnp.where(adm, sc, -jnp.inf)
            lo0 = jnp.min(jnp.where(adm, sc, jnp.inf), axis=-1, keepdims=True)
            hi0 = jnp.max(jnp.where(adm, sc, -jnp.inf), axis=-1, keepdims=True)
            thr = _kth_largest(x_ref, None, kf, lo0, hi0)
            _topk_mask(x_ref, m_ref, tri_ref, None, kf, thr)

            def head(h, carry):
                hr = _head_rows(h)
                far = rb_ref[FAR_BUCKET, h]
                x_ref[...] = jnp.dot(q8_ref[h], kt_ref[hr, :w], preferred_element_type=F32) + (m_ref[...] + far)
                sc_ref[:, pl.ds(d0, tq)] += bias_ref[h, 0] - far

                @pl.when(i > 0)
                def _():
                    sc_ref[:, pl.ds(d1, tq)] += bias_ref[h, 1] - far

                lg = x_ref[...]
                p = jnp.exp(lg - jnp.max(lg, axis=-1, keepdims=True))
                den = jnp.sum(p, axis=-1, keepdims=True)
                o8_ref[h] = lax.dot_general(p.astype(BF16), vt_ref[hr, :w], NT, preferred_element_type=F32) / den
                return carry

            lax.fori_loop(0, N_HEADS, head, 0)

    for h in range(N_HEADS):
        hs = _head_rows(h)
        o_ref[:, hs] = (o8_ref[h] * sz_ref[:, hs].astype(F32)).astype(BF16)


def _dsa_prompt(pr, rel_bias, tri, batch, seq):
    tq = LANES
    assert seq % tq == 0
    nq = seq // tq
    topk = min(TOPK_MAX, seq // 4)
    step, widths = _causal_widths(seq)
    rowspec = lambda n: pl.BlockSpec((tq, n), lambda b, i: (b * nq + i, 0))
    seqspec = lambda n: pl.BlockSpec((None, n, seq), lambda b, i: (b, 0, 0))
    return pl.pallas_call(
        functools.partial(_dsa_prompt_body, topk=topk, tq=tq, step=step, widths=widths),
        grid=(batch, nq),
        in_specs=[pl.BlockSpec(memory_space=pltpu.SMEM), rowspec(HW), rowspec(HW), rowspec(N_HEADS), rowspec(HW),
                  seqspec(D_IDX), seqspec(HW), seqspec(HW), pl.BlockSpec(tri.shape, lambda b, i: (0, 0))],
        out_specs=rowspec(HW),
        out_shape=jax.ShapeDtypeStruct((batch * seq, HW), BF16),
        scratch_shapes=[pltpu.VMEM((N_HEADS, 2, tq, tq), F32), pltpu.VMEM((tq, seq), F32),
                        pltpu.VMEM((tq, seq), F32), pltpu.VMEM((N_HEADS, tq, HEAD_DIM), BF16),
                        pltpu.VMEM((N_HEADS, tq, HEAD_DIM), F32)],
        compiler_params=_cparams(2, VMEM_LIMIT),
        name="dsa_prompt",
    )(rel_bias, pr["qa16"], pr["iq16"], pr["iw"], pr["sza"], pr["ikt16"], pr["kat16"], pr["vat16"], tri)


def _merge_body(x_ref, ua_ref, ub_ref, sga_ref, sgb_ref, wua_ref, wub_ref, wo_ref, y_ref):
    ya = jnp.dot(ua_ref[...], wua_ref[...], preferred_element_type=F32)
    yb = jnp.dot(ub_ref[...], wub_ref[...], preferred_element_type=F32)
    m = sga_ref[...].astype(F32) * ya + sgb_ref[...].astype(F32) * yb
    y_ref[...] = x_ref[...] + jnp.dot(m.astype(BF16), wo_ref[...], preferred_element_type=F32)


def _merge(x2d, ua, ub, sga, sgb, wts):
    rows, d_model = x2d.shape
    tm = min(512, rows)
    assert rows % tm == 0
    row = lambda n: pl.BlockSpec((tm, n), lambda i: (i, 0))
    full = lambda a: pl.BlockSpec(a.shape, lambda i: (0, 0))
    return pl.pallas_call(
        _merge_body,
        grid=(rows // tm,),
        in_specs=[row(d_model), row(HW), row(HW), row(d_model), row(d_model),
                  full(wts["wua"]), full(wts["wub"]), full(wts["wo"])],
        out_specs=row(d_model),
        out_shape=jax.ShapeDtypeStruct((rows, d_model), F32),
        compiler_params=_cparams(1, VMEM_LIMIT),
        name="merge",
    )(x2d, ua, ub, sga, sgb, wts["wua"], wts["wub"], wts["wo"])


def _idx_sample_body(pt_ref, iq_ref, iw_ref, *refs):
    pages, o_ref = refs[:IDX_PAGES], refs[IDX_PAGES]
    iq = iq_ref[...]
    iw = iw_ref[...]
    rows = []
    for pg in pages:
        r = jnp.dot(iq, pg[...].astype(BF16), preferred_element_type=F32)
        rows.append(jnp.sum(jnp.maximum(r, 0.0) * iw, axis=0, keepdims=True))
    o_ref[...] = jnp.concatenate(rows, axis=0)


def _idx_sample(iq3, iw3, cache_ikt, page_table):
    bd, n_pages = page_table.shape
    assert n_pages % IDX_PAGES == 0
    ng = n_pages // IDX_PAGES
    page_specs = [pl.BlockSpec((None, D_IDX, PAGE), lambda b, g, pt, i=i: (pt[b, g * IDX_PAGES + i], 0, 0))
                  for i in range(IDX_PAGES)]
    return pl.pallas_call(
        _idx_sample_body,
        grid_spec=pltpu.PrefetchScalarGridSpec(
            num_scalar_prefetch=1, grid=(bd, ng),
            in_specs=[pl.BlockSpec((None, QPAD, D_IDX), lambda b, g, pt: (b, 0, 0)),
                      pl.BlockSpec((None, QPAD, 1), lambda b, g, pt: (b, 0, 0))] + page_specs,
            out_specs=pl.BlockSpec((None, IDX_PAGES, PAGE), lambda b, g, pt: (b, g, 0))),
        out_shape=jax.ShapeDtypeStruct((bd, n_pages, PAGE), F32),
        compiler_params=_cparams(2),
        name="idx_sample",
    )(page_table, iq3, iw3, *([cache_ikt] * IDX_PAGES))


def _select_sample_body(sc_ref, iq_ref, ik_ref, iw_ref, tri_ref, madd_ref, maddn_ref, *, topk):
    bd = sc_ref.shape[0]
    ii = lax.broadcasted_iota(I32, (bd, bd), 0)
    jj = lax.broadcasted_iota(I32, (bd, bd), 1)
    ik = ik_ref[...]
    snew = jnp.zeros((bd, 1), F32)
    for h in range(N_HEADS):
        r = lax.dot_general(iq_ref[:, h * D_IDX:(h + 1) * D_IDX], ik, NT, preferred_element_type=F32)
        rd = jnp.sum(jnp.where(ii == jj, r, 0.0), axis=-1, keepdims=True)
        snew = snew + jnp.maximum(rd, 0.0) * iw_ref[:, h:h + 1]
    snew = snew + 0.0
    x = sc_ref[...]
    kf = jnp.full((bd, 1), float(topk), F32)
    lo0 = jnp.minimum(jnp.min(x, axis=-1, keepdims=True), snew)
    hi0 = jnp.maximum(jnp.max(x, axis=-1, keepdims=True), snew)
    thr = _kth_largest(sc_ref, snew, kf, lo0, hi0)
    mn = _topk_mask(sc_ref, madd_ref, tri_ref, snew, kf, thr)
    maddn_ref[...] = jnp.broadcast_to(mn, maddn_ref.shape)


def _select_sample(scores2d, iq16, ik16, iw, tri, n_new):
    bd, past = scores2d.shape
    topk = min(TOPK_MAX, (past + n_new) // 4)
    full = lambda a: pl.BlockSpec(a.shape, lambda i: (0,) * a.ndim)
    return pl.pallas_call(
        functools.partial(_select_sample_body, topk=topk),
        grid=(1,),
        in_specs=[full(scores2d), full(iq16), full(ik16), full(iw), full(tri)],
        out_specs=[pl.BlockSpec((bd, past), lambda i: (0, 0)), pl.BlockSpec((bd, LANES), lambda i: (0, 0))],
        out_shape=[jax.ShapeDtypeStruct((bd, past), F32), jax.ShapeDtypeStruct((bd, LANES), F32)],
        compiler_params=_cparams(1, VMEM_LIMIT),
        name="select_sample",
    )(scores2d, iq16, ik16, iw, tri)


def _head_mask():
    r = lax.broadcasted_iota(I32, (QPAD, HW), 0)
    c = lax.broadcasted_iota(I32, (QPAD, HW), 1)
    return jnp.where((c >= r * HEAD_DIM) & (c < (r + 1) * HEAD_DIM), 1.0, 0.0)


def _online_update(m_ref, l_ref, acc_ref, logits, vt_pages):
    m_old = m_ref[...]
    m_new = jnp.maximum(m_old, jnp.max(logits, axis=-1, keepdims=True))
    p = jnp.exp(logits - m_new)
    corr = jnp.exp(m_old - m_new)
    l_ref[...] = l_ref[...] * corr + jnp.sum(p, axis=-1, keepdims=True)
    pv = jnp.zeros(acc_ref.shape, F32)
    for i, vp in enumerate(vt_pages):
        pv = pv + lax.dot_general(p[:, i * PAGE:(i + 1) * PAGE].astype(BF16), vp[...].astype(BF16), NT,
                                  preferred_element_type=F32)
    acc_ref[...] = acc_ref[...] * corr + pv
    m_ref[...] = m_new


def _finish_new_token(m_ref, l_ref, acc_ref, logit_new, vnew_ref, hmask, sz_ref, o_ref):
    m_old = m_ref[...]
    m_new = jnp.maximum(m_old, logit_new)
    p_new = jnp.exp(logit_new - m_new)
    corr = jnp.exp(m_old - m_new)
    den = l_ref[...] * corr + p_new
    o = (acc_ref[...] * corr + p_new * vnew_ref[...]) / den
    row = jnp.sum(o * hmask, axis=0, keepdims=True)
    o_ref[...] = (row * sz_ref[...].astype(F32)).astype(BF16)


def _dsa_sample_body(pt_ref, rbt_ref, q_ref, knew_ref, vnew_ref, sz_ref, madd_ref, maddn_ref, *refs):
    n = ATTN_PAGES
    kt_pages, vt_pages = refs[:n], refs[n:2 * n]
    o_ref, m_ref, l_ref, acc_ref, blast_ref = refs[2 * n:]
    b = pl.program_id(0)
    g = pl.program_id(1)
    last = g == pl.num_programs(1) - 1
    hmask = _head_mask()
    qbd = (q_ref[...].astype(F32) * hmask).astype(BF16)

    @pl.when((b == 0) & (g == 0))
    def _last_page_bias():
        bucket = _t5_bucket(PAGE - lax.broadcasted_iota(I32, (QPAD, PAGE), 1))
        acc = jnp.zeros((QPAD, PAGE), F32)
        for kb in range(N_BUCKETS):
            acc = jnp.where(bucket == kb, rbt_ref[:, kb:kb + 1], acc)
        blast_ref[...] = acc

    @pl.when(g == 0)
    def _init():
        m_ref[...] = jnp.full(m_ref.shape, NEG, F32)
        l_ref[...] = jnp.zeros(l_ref.shape, F32)
        acc_ref[...] = jnp.zeros(acc_ref.shape, F32)

    far = rbt_ref[:, FAR_BUCKET:FAR_BUCKET + 1]
    parts = []
    for i in range(n):
        lt = jnp.dot(qbd, kt_pages[i][...].astype(BF16), preferred_element_type=F32)
        bias = far if i < n - 1 else jnp.where(last, blast_ref[...], jnp.broadcast_to(far, (QPAD, PAGE)))
        parts.append(lt + madd_ref[i:i + 1, :] + bias)
    _online_update(m_ref, l_ref, acc_ref, jnp.concatenate(parts, axis=1), vt_pages)

    @pl.when(last)
    def _finish():
        ln = jnp.sum(qbd.astype(F32) * knew_ref[...], axis=-1, keepdims=True)
        ln = ln + rbt_ref[:, 0:1] + maddn_ref[:, 0:1]
        _finish_new_token(m_ref, l_ref, acc_ref, ln, vnew_ref, hmask, sz_ref, o_ref)


def _fox_sample_body(pt_ref, q_ref, knew_ref, vnew_ref, sz_ref, fn_ref, *refs):
    n = ATTN_PAGES
    kt_pages, vt_pages, lf_pages = refs[:n], refs[n:2 * n], refs[2 * n:3 * n]
    o_ref, m_ref, l_ref, acc_ref, c_ref = refs[3 * n:]
    g = pl.program_id(1)
    last = g == pl.num_programs(1) - 1
    hmask = _head_mask()
    qbd = (q_ref[...].astype(F32) * hmask).astype(BF16)

    @pl.when(g == 0)
    def _init():
        m_ref[...] = jnp.full(m_ref.shape, NEG, F32)
        l_ref[...] = jnp.zeros(l_ref.shape, F32)
        acc_ref[...] = jnp.zeros(acc_ref.shape, F32)
        c_ref[...] = jnp.zeros(c_ref.shape, F32)

    lf = jnp.concatenate([pg[...] for pg in lf_pages], axis=0)
    lane = lax.broadcasted_iota(I32, lf.shape, 1)
    suf = lf
    k = 1
    while k < PAGE:
        suf = suf + jnp.where(lane + k < PAGE, pltpu.roll(suf, PAGE - k, axis=1), 0.0)
        k *= 2
    later = c_ref[...]
    zpad = jnp.zeros((QPAD - N_HEADS, PAGE), F32)
    parts = []
    for i in range(n):
        rs = slice(i * N_HEADS, (i + 1) * N_HEADS)
        lt = jnp.dot(qbd, kt_pages[i][...].astype(BF16), preferred_element_type=F32)
        bias = (suf[rs] - lf[rs]) + (later + fn_ref[...])[:N_HEADS]
        parts.append(lt + jnp.concatenate([bias, zpad], axis=0))
        later = later + jnp.concatenate([suf[rs, 0:1], zpad[:, 0:1]], axis=0)
    c_ref[...] = later
    _online_update(m_ref, l_ref, acc_ref, jnp.concatenate(parts, axis=1), vt_pages)

    @pl.when(last)
    def _finish():
        ln = jnp.sum(qbd.astype(F32) * knew_ref[...], axis=-1, keepdims=True)
        _finish_new_token(m_ref, l_ref, acc_ref, ln, vnew_ref, hmask, sz_ref, o_ref)


def _row3(a):
    return a.reshape(a.shape[0], 1, a.shape[1])


def _sample_attention(body, name, page_table, small_inputs, small_specs, paged_inputs, paged_specs, bd, ng,
                      extra_scratch):
    rowspec = pl.BlockSpec((None, 1, HW), lambda b, g, pt: (b, 0, 0))
    out = pl.pallas_call(
        body,
        grid_spec=pltpu.PrefetchScalarGridSpec(
            num_scalar_prefetch=1, grid=(bd, ng),
            in_specs=small_specs + paged_specs,
            out_specs=rowspec,
            scratch_shapes=[pltpu.VMEM((QPAD, 1), F32), pltpu.VMEM((QPAD, 1), F32), pltpu.VMEM((QPAD, HW), F32),
                            extra_scratch]),
        out_shape=jax.ShapeDtypeStruct((bd, 1, HW), BF16),
        compiler_params=_cparams(2, VMEM_LIMIT),
        name=name,
    )(page_table, *small_inputs, *paged_inputs)
    return out.reshape(bd, HW)


def _dsa_sample(sp, knew, vnew, rbt, madd3, maddn, cache_kt, cache_vt, page_table):
    bd, n_pages = page_table.shape
    n = ATTN_PAGES
    assert n_pages % n == 0
    ng = n_pages // n
    rowspec = pl.BlockSpec((None, 1, HW), lambda b, g, pt: (b, 0, 0))
    page = lambda i: pl.BlockSpec((None, HW, PAGE), lambda b, g, pt, i=i: (pt[b, g * n + i], 0, 0))
    small_inputs = [rbt, _row3(sp["qa16"]), _row3(knew), _row3(vnew), _row3(sp["sza"]), madd3, _row3(maddn)]
    small_specs = [pl.BlockSpec(rbt.shape, lambda b, g, pt: (0, 0)), rowspec, rowspec, rowspec, rowspec,
                   pl.BlockSpec((None, n, PAGE), lambda b, g, pt: (b, g, 0)),
                   pl.BlockSpec((None, 1, LANES), lambda b, g, pt: (b, 0, 0))]
    return _sample_attention(_dsa_sample_body, "dsa_sample", page_table, small_inputs, small_specs,
                             [cache_kt] * n + [cache_vt] * n, [page(i) for i in range(n)] * 2, bd, ng,
                             pltpu.VMEM((QPAD, PAGE), F32))


def _fox_sample(sp, knew, vnew, fn3, cache_kt, cache_vt, cache_lft, page_table):
    bd, n_pages = page_table.shape
    n = ATTN_PAGES
    assert n_pages % n == 0
    ng = n_pages // n
    rowspec = pl.BlockSpec((None, 1, HW), lambda b, g, pt: (b, 0, 0))
    rev = lambda b, g, pt, i: pt[b, n_pages - 1 - (g * n + i)]
    page = lambda i: pl.BlockSpec((None, HW, PAGE), lambda b, g, pt, i=i: (rev(b, g, pt, i), 0, 0))
    lfpage = lambda i: pl.BlockSpec((None, N_HEADS, PAGE), lambda b, g, pt, i=i: (rev(b, g, pt, i), 0, 0))
    small_inputs = [_row3(sp["qb16"]), _row3(knew), _row3(vnew), _row3(sp["szb"]), fn3]
    small_specs = [rowspec, rowspec, rowspec, rowspec, pl.BlockSpec((None, QPAD, 1), lambda b, g, pt: (b, 0, 0))]
    return _sample_attention(_fox_sample_body, "fox_sample", page_table, small_inputs, small_specs,
                             [cache_kt] * n + [cache_vt] * n + [cache_lft] * n,
                             [page(i) for i in range(n)] * 2 + [lfpage(i) for i in range(n)], bd, ng,
                             pltpu.VMEM((QPAD, 1), F32))


def _prepare_weights(d_model, g_norm, w_in, b_fgate, qn_a, kn_a, qn_b, kn_b, w_up_a, w_up_b, w_out):
    widths = (HW, HW, HW, HW, N_HEADS * D_IDX, N_HEADS, D_IDX, HW, HW, HW, HW, N_HEADS, d_model, d_model)
    offs = [0]
    for w in widths:
        offs.append(offs[-1] + w)
    wt = w_in.T
    seg = lambda a, b: wt[offs[a]:offs[b]]
    zeros = lambda n: jnp.zeros((n, d_model), w_in.dtype)
    w_rows = jnp.concatenate([seg(0, 1), seg(3, 4), seg(4, 5), seg(7, 8), seg(10, 11), seg(12, 14),
                              seg(5, 6), zeros(LANES - N_HEADS)], axis=0)
    w_cols = jnp.concatenate([seg(1, 3), seg(8, 10), seg(6, 7), seg(11, 12),
                              zeros(LANES - D_IDX - N_HEADS)], axis=0)
    tile = lambda v: jnp.tile(v.astype(F32), N_HEADS)
    lanes = lambda v: jnp.broadcast_to(tile(v)[:, None], (HW, LANES))
    blk = jnp.arange(HW) // HEAD_DIM
    return {
        "g": g_norm.astype(F32).reshape(1, d_model),
        "wr": w_rows.astype(BF16), "wt": w_cols.astype(BF16),
        "bft": b_fgate.astype(F32).reshape(N_HEADS, 1),
        "nrm": jnp.stack([tile(qn_a), tile(qn_b)]),
        "nrmt": jnp.stack([lanes(kn_a), lanes(kn_b)]),
        "gm": jnp.where(blk[:, None] == blk[None, :], 1.0 / HEAD_DIM, 0.0).astype(BF16),
        "wua": w_up_a.astype(BF16), "wub": w_up_b.astype(BF16), "wo": w_out.astype(BF16),
    }


def _pad_rows(a, rows):
    return jnp.pad(a, ((0, 0), (0, rows - a.shape[1])) + ((0, 0),) * (a.ndim - 2))


def _pages_feature_major(cache):
    n_pool = cache.shape[0]
    flat = cache.reshape(n_pool, PAGE, -1)
    return jnp.swapaxes(flat, 1, 2)


def _heads_state(xt, lead):
    b, _, s = xt.shape
    return jnp.transpose(xt.reshape(b, N_HEADS, HEAD_DIM, s), (0, 3, 1, 2)).reshape(*lead, N_HEADS, HEAD_DIM)


def kernel(x_prompt, x_sample, cache_a_k, cache_a_v, cache_a_idx_k, cache_b_k, cache_b_v, cache_b_logf, page_table,
           rel_bias, g_norm, w_in, b_fgate, qnorm_a, knorm_a, qnorm_b, knorm_b, w_up_a, w_up_b, w_out):
    batch, seq, d_model = x_prompt.shape
    bd, t_new, _ = x_sample.shape
    depth, n_pool = cache_a_k.shape[:2]
    n_pages = page_table.shape[1]
    past = n_pages * PAGE
    assert t_new == 1, "the decode kernels handle one new token per sequence"
    assert cache_a_k.shape[2:] == (PAGE, N_HEADS, HEAD_DIM) and bd % LANES == 0

    rel_bias = rel_bias.astype(F32)
    rbt = jnp.pad(rel_bias.T, ((0, QPAD - N_HEADS), (0, 0)))
    tri = jnp.triu(jnp.ones((LANES, LANES), F32)).astype(BF16)

    xp = x_prompt.reshape(batch * seq, d_model)
    xs = x_sample.reshape(bd * t_new, d_model)
    st_p, st_s = [], []
    for layer in range(depth):
        wts = _prepare_weights(d_model, g_norm[layer], w_in[layer], b_fgate[layer], qnorm_a[layer], knorm_a[layer],
                               qnorm_b[layer], knorm_b[layer], w_up_a[layer], w_up_b[layer], w_out[layer])
        pr = _project(xp, wts, batch, seq)
        ft = _cumsum_lanes(pr["lft"], tri)
        ub = _fox_prompt(pr, ft, batch, seq)
        ua = _dsa_prompt(pr, rel_bias, tri, batch, seq)
        lead = (batch, seq)
        st_p.append((_heads_state(pr["kat"], lead), _heads_state(pr["vat"], lead), jnp.swapaxes(pr["ikt"], 1, 2),
                     _heads_state(pr["kbt"], lead), _heads_state(pr["vbt"], lead), jnp.swapaxes(pr["lft"], 1, 2)))
        xp = _merge(xp, ua, ub, pr["sga"], pr["sgb"], wts)

        sp = _project(xs, wts, 1, bd)
        rows_of = lambda name: sp[name][0].T
        iw_s, ik16_s, lf_s = sp["iw"], rows_of("ikt16"), rows_of("lft")
        iq3 = _pad_rows(sp["iq16"].reshape(bd, N_HEADS, D_IDX), QPAD)
        iw3 = _pad_rows(iw_s.reshape(bd, N_HEADS, 1), QPAD)
        scores = _idx_sample(iq3, iw3, _pages_feature_major(cache_a_idx_k[layer]), page_table)
        madd, maddn = _select_sample(scores.reshape(bd, past), sp["iq16"], ik16_s, iw_s, tri, t_new)
        ua_s = _dsa_sample(sp, rows_of("kat"), rows_of("vat"), rbt, madd.reshape(bd, n_pages, PAGE), maddn,
                           _pages_feature_major(cache_a_k[layer]), _pages_feature_major(cache_a_v[layer]), page_table)
        fn3 = _pad_rows(lf_s.reshape(bd, N_HEADS, 1), QPAD)
        ub_s = _fox_sample(sp, rows_of("kbt"), rows_of("vbt"), fn3, _pages_feature_major(cache_b_k[layer]),
                           _pages_feature_major(cache_b_v[layer]), _pages_feature_major(cache_b_logf[layer]),
                           page_table)
        lead = (bd, t_new)
        st_s.append((_heads_state(sp["kat"], lead), _heads_state(sp["vat"], lead),
                     rows_of("ikt").reshape(bd, t_new, D_IDX),
                     _heads_state(sp["kbt"], lead), _heads_state(sp["vbt"], lead), lf_s.reshape(bd, t_new, N_HEADS)))
        xs = _merge(xs, ua_s, ub_s, sp["sga"], sp["sgb"], wts)

    outs_p = [jnp.stack(z) for z in zip(*st_p)]
    outs_s = [jnp.stack(z) for z in zip(*st_s)]
    return (xp.reshape(batch, seq, d_model), xs.reshape(bd, t_new, d_model), *outs_p, *outs_s)
```

```python
import functools
import math

import jax
import jax.numpy as jnp
from jax import lax
from jax.experimental import pallas as pl
from jax.experimental.pallas import tpu as pltpu

F32 = jnp.float32
BF16 = jnp.bfloat16
I32 = jnp.int32

HEAD_DIM = 64
N_HEADS = 8
HW = N_HEADS * HEAD_DIM
D_IDX = 64
PAGE = 128
TOPK_MAX = 256
N_BUCKETS = 32
MAX_EXACT = N_BUCKETS // 2
MAX_DISTANCE = 128
FAR_BUCKET = N_BUCKETS - 1
EPS = 1e-6
NEG = -1e30
SCALE = HEAD_DIM ** -0.5
IDX_W_SCALE = (N_HEADS * D_IDX) ** -0.5

LANES = 128
ATTN_PAGES = 16
IDX_PAGES = 32
QPAD = 16
BISECT_TRIPS = N_HEADS
CAUSAL_STEP = 512
VMEM_LIMIT = 56 * 1024 * 1024

NT = (((1,), (1,)), ((), ()))


def _cparams(n_axes, vmem=None):
    return pltpu.CompilerParams(dimension_semantics=("arbitrary",) * n_axes, vmem_limit_bytes=vmem)


def _log_sigmoid(z):
    return -(jnp.maximum(-z, 0.0) + jnp.log1p(jnp.exp(-jnp.abs(z))))


def _t5_bucket(dist):
    d = jnp.maximum(dist, 0)
    df = jnp.maximum(d, 1).astype(F32)
    large = MAX_EXACT + (jnp.log(df / MAX_EXACT) / math.log(MAX_DISTANCE / MAX_EXACT)
                         * (N_BUCKETS - MAX_EXACT)).astype(I32)
    large = jnp.minimum(large, N_BUCKETS - 1)
    return jnp.where(d < MAX_EXACT, d, large)


def _head_rows(h):
    if isinstance(h, int):
        return slice(h * HEAD_DIM, (h + 1) * HEAD_DIM)
    return pl.ds(pl.multiple_of(h * HEAD_DIM, HEAD_DIM), HEAD_DIM)


def _causal_widths(seq):
    step = CAUSAL_STEP if seq % CAUSAL_STEP == 0 else seq
    return step, [step * (t + 1) for t in range(seq // step)]


def _project_body(x_ref, g_ref, wr_ref, wt_ref, bft_ref, nrm_ref, nrmt_ref, gm_ref,
                  qa_o, sza_o, iq_o, iw_o, qb_o, szb_o, sga_o, sgb_o,
                  kat_o, vat_o, kbt_o, vbt_o, kat16_o, vat16_o, kbt16_o, vbt16_o, ikt_o, ikt16_o, lft_o,
                  *, d_model, tm):
    x = x_ref[...]
    ms = jnp.mean(x * x, axis=-1, keepdims=True)
    h = ((x * lax.rsqrt(ms + EPS)) * g_ref[...]).astype(BF16)
    gm = gm_ref[...]
    nd = d_model // HW

    def rows(i, n=HW):
        return lax.dot_general(h, wr_ref[i * HW:i * HW + n, :], NT, preferred_element_type=F32)

    def cols(i, n=HW):
        return lax.dot_general(wt_ref[i * HW:i * HW + n, :], h, NT, preferred_element_type=F32)

    def head_norm(p, row):
        ss = jnp.dot((p * p).astype(BF16), gm, preferred_element_type=F32)
        return (p * lax.rsqrt(ss + EPS)) * nrm_ref[row:row + 1, :]

    def head_norm_t(pt, j):
        ss = jnp.dot(gm, (pt * pt).astype(BF16), preferred_element_type=F32)
        return (pt * lax.rsqrt(ss + EPS)) * jnp.concatenate([nrmt_ref[j]] * (tm // LANES), axis=1)

    def silu(z):
        return z * jax.nn.sigmoid(z)

    qa_o[...] = (head_norm(rows(0), 0) * SCALE).astype(BF16)
    sza_o[...] = silu(rows(1)).astype(BF16)
    iq_o[...] = rows(2).astype(BF16)
    qb_o[...] = (head_norm(rows(3), 1) * SCALE).astype(BF16)
    szb_o[...] = silu(rows(4)).astype(BF16)
    for c in range(nd):
        sga_o[:, c * HW:(c + 1) * HW] = jax.nn.sigmoid(rows(5 + c)).astype(BF16)
        sgb_o[:, c * HW:(c + 1) * HW] = jax.nn.sigmoid(rows(5 + nd + c)).astype(BF16)
    iw_o[...] = rows(5 + 2 * nd, LANES)[:, 0:N_HEADS] * IDX_W_SCALE

    kat = head_norm_t(cols(0), 0)
    kat_o[...] = kat
    kat16_o[...] = kat.astype(BF16)
    vat = cols(1)
    vat_o[...] = vat
    vat16_o[...] = vat.astype(BF16)
    kbt = head_norm_t(cols(2), 1)
    kbt_o[...] = kbt
    kbt16_o[...] = kbt.astype(BF16)
    vbt = cols(3)
    vbt_o[...] = vbt
    vbt16_o[...] = vbt.astype(BF16)
    small = cols(4, LANES)
    ikt = small[0:D_IDX]
    ikt_o[...] = ikt
    ikt16_o[...] = ikt.astype(BF16)
    lft_o[...] = _log_sigmoid(small[D_IDX:D_IDX + N_HEADS] + bft_ref[...])


def _project(x2d, wts, batch, seq):
    rows, d_model = x2d.shape
    assert rows == batch * seq
    tm = min(256, seq)
    assert seq % tm == 0 and tm % LANES == 0 and d_model % HW == 0
    nt = seq // tm
    consts = (wts["g"], wts["wr"], wts["wt"], wts["bft"], wts["nrm"], wts["nrmt"], wts["gm"])

    def row(n, dt):
        return jax.ShapeDtypeStruct((rows, n), dt), pl.BlockSpec((tm, n), lambda b, i: (b * nt + i, 0))

    def col(n, dt):
        return (jax.ShapeDtypeStruct((batch, n, seq), dt), pl.BlockSpec((None, n, tm), lambda b, i: (b, 0, i)))

    outs = [row(HW, BF16), row(HW, BF16), row(HW, BF16), row(N_HEADS, F32), row(HW, BF16), row(HW, BF16),
            row(d_model, BF16), row(d_model, BF16),
            col(HW, F32), col(HW, F32), col(HW, F32), col(HW, F32),
            col(HW, BF16), col(HW, BF16), col(HW, BF16), col(HW, BF16),
            col(D_IDX, F32), col(D_IDX, BF16), col(N_HEADS, F32)]
    names = ("qa16", "sza", "iq16", "iw", "qb16", "szb", "sga", "sgb",
             "kat", "vat", "kbt", "vbt", "kat16", "vat16", "kbt16", "vbt16", "ikt", "ikt16", "lft")
    res = pl.pallas_call(
        functools.partial(_project_body, d_model=d_model, tm=tm),
        grid=(batch, nt),
        in_specs=[pl.BlockSpec((tm, d_model), lambda b, i: (b * nt + i, 0))]
        + [pl.BlockSpec(c.shape, lambda b, i, n=c.ndim: (0,) * n) for c in consts],
        out_specs=[o[1] for o in outs],
        out_shape=[o[0] for o in outs],
        compiler_params=_cparams(2, VMEM_LIMIT),
        name="project",
    )(x2d, *consts)
    return dict(zip(names, res))


def _split3(a):
    hi = a.astype(BF16)
    r1 = a - hi.astype(F32)
    mid = r1.astype(BF16)
    lo = (r1 - mid.astype(F32)).astype(BF16)
    return hi, mid, lo


def _cumsum_body(x_ref, tri_ref, o_ref, *, seq):
    tri = tri_ref[...]
    carry = jnp.zeros((x_ref.shape[0], 1), F32)
    for c in range(seq // LANES):
        hi, mid, lo = _split3(x_ref[:, c * LANES:(c + 1) * LANES])
        cs = (jnp.dot(hi, tri, preferred_element_type=F32) + jnp.dot(mid, tri, preferred_element_type=F32)
              + jnp.dot(lo, tri, preferred_element_type=F32)) + carry
        o_ref[:, c * LANES:(c + 1) * LANES] = cs
        carry = cs[:, LANES - 1:LANES]


def _cumsum_lanes(xt, tri):
    batch, n, seq = xt.shape
    spec = pl.BlockSpec((None, n, seq), lambda b: (b, 0, 0))
    return pl.pallas_call(
        functools.partial(_cumsum_body, seq=seq),
        grid=(batch,),
        in_specs=[spec, pl.BlockSpec(tri.shape, lambda b: (0, 0))],
        out_specs=spec,
        out_shape=jax.ShapeDtypeStruct(xt.shape, F32),
        compiler_params=_cparams(1),
        name="cumsum",
    )(xt, tri)


def _fox_prompt_body(q_ref, kt_ref, vt_ref, ft_ref, sz_ref, o_ref, q8_ref, o8_ref, *, tq, step, widths):
    i = pl.program_id(1)
    for h in range(N_HEADS):
        q8_ref[h] = q_ref[:, _head_rows(h)]
    tier = ((i + 1) * tq + step - 1) // step - 1

    for t, w in enumerate(widths):
        @pl.when(tier == t)
        def _(w=w):
            row_pos = i * tq + lax.broadcasted_iota(I32, (tq, step), 0)
            col_pos = (w - step) + lax.broadcasted_iota(I32, (tq, step), 1)
            visible = col_pos <= row_pos

            def head(h):
                hr = _head_rows(h)
                s = jnp.dot(q8_ref[h], kt_ref[hr, :w], preferred_element_type=F32) - ft_ref[pl.ds(h, 1), :w]
                tail = jnp.where(visible, s[:, w - step:], NEG)
                s = tail if w == step else jnp.concatenate([s[:, :w - step], tail], axis=1)
                p = jnp.exp(s - jnp.max(s, axis=-1, keepdims=True))
                den = jnp.sum(p, axis=-1, keepdims=True)
                o8_ref[h] = lax.dot_general(p.astype(BF16), vt_ref[hr, :w], NT, preferred_element_type=F32) / den

            def pair(j, carry):
                head(2 * j)
                head(2 * j + 1)
                return carry

            lax.fori_loop(0, N_HEADS // 2, pair, 0)

    for h in range(N_HEADS):
        hs = _head_rows(h)
        o_ref[:, hs] = (o8_ref[h] * sz_ref[:, hs].astype(F32)).astype(BF16)


def _fox_prompt(pr, ft, batch, seq):
    tq = min(256, seq)
    nq = seq // tq
    step, widths = _causal_widths(seq)
    assert step % tq == 0
    rowspec = pl.BlockSpec((tq, HW), lambda b, i: (b * nq + i, 0))
    seqspec = lambda n: pl.BlockSpec((None, n, seq), lambda b, i: (b, 0, 0))
    return pl.pallas_call(
        functools.partial(_fox_prompt_body, tq=tq, step=step, widths=widths),
        grid=(batch, nq),
        in_specs=[rowspec, seqspec(HW), seqspec(HW), seqspec(N_HEADS), rowspec],
        out_specs=rowspec,
        out_shape=jax.ShapeDtypeStruct((batch * seq, HW), BF16),
        scratch_shapes=[pltpu.VMEM((N_HEADS, tq, HEAD_DIM), BF16), pltpu.VMEM((N_HEADS, tq, HEAD_DIM), F32)],
        compiler_params=_cparams(2, VMEM_LIMIT),
        name="fox_prompt",
    )(pr["qb16"], pr["kbt16"], pr["vbt16"], ft, pr["szb"])


def _count(pred):
    return jnp.sum(jnp.where(pred, 1.0, 0.0), axis=-1, keepdims=True)


def _kth_largest(x_ref, extra, kf, lo0, hi0, side_work=None):
    def cnt_gt(v):
        c = _count(x_ref[...] > v)
        return c if extra is None else c + jnp.where(extra > v, 1.0, 0.0)

    def cnt_ge(v):
        c = _count(x_ref[...] >= v)
        return c if extra is None else c + jnp.where(extra >= v, 1.0, 0.0)

    def max_where(pred_fn):
        x = x_ref[...]
        m = jnp.max(jnp.where(pred_fn(x), x, -jnp.inf), axis=-1, keepdims=True)
        return m if extra is None else jnp.maximum(m, jnp.where(pred_fn(extra), extra, -jnp.inf))

    def bisect(_, c):
        lo, hi = c
        mid = 0.5 * (lo + hi)
        above = cnt_gt(mid) >= kf
        return jnp.where(above, mid, lo), jnp.where(above, hi, mid)

    def trip(j, c):
        if side_work is not None:
            side_work(j)
        return bisect(j, bisect(j, c))

    lo, hi = lax.fori_loop(0, BISECT_TRIPS, trip, (lo0, hi0))

    def finish(c):
        lo, hi, _ = c
        lo, hi = lax.fori_loop(0, 4, bisect, (lo, hi), unroll=True)
        t1 = max_where(lambda x: x <= hi)
        ok = cnt_ge(t1) >= kf
        t2 = max_where(lambda x: x < t1)
        pending = jnp.sum(jnp.where(ok, 0.0, 1.0))
        return jnp.where(ok, t1, lo), jnp.where(ok, t1, t2), pending

    _, hi, _ = lax.while_loop(lambda c: c[2] > 0.0, finish, (lo, hi, jnp.float32(1.0)))
    return hi


def _topk_mask(x_ref, madd_ref, tri_ref, extra, kf, thr):
    x = x_ref[...]
    c_gt = _count(x > thr)
    c_ge = _count(x >= thr)
    if extra is not None:
        c_gt = c_gt + jnp.where(extra > thr, 1.0, 0.0)
        c_ge = c_ge + jnp.where(extra >= thr, 1.0, 0.0)
    need = kf - c_gt
    madd_ref[...] = jnp.where(x >= thr, 0.0, NEG)
    surplus = jnp.sum(jnp.where(c_ge > kf, 1.0, 0.0))

    @pl.when(surplus > 0.0)
    def _():
        tri = tri_ref[...]
        seen = jnp.zeros_like(thr)
        for c in range(x_ref.shape[1] // LANES):
            cs = slice(c * LANES, (c + 1) * LANES)
            xc = x_ref[:, cs]
            tie = jnp.where(xc == thr, 1.0, 0.0)
            rank = jnp.dot(tie.astype(BF16), tri, preferred_element_type=F32) + seen
            madd_ref[:, cs] = jnp.where(xc > thr, 0.0, jnp.where(xc == thr, jnp.where(rank <= need, 0.0, NEG), NEG))
            seen = rank[:, LANES - 1:LANES]

    if extra is None:
        return None
    ties_before = c_ge - c_gt - jnp.where(extra == thr, 1.0, 0.0)
    return jnp.where(extra > thr, 0.0, jnp.where(extra == thr, jnp.where(ties_before + 1.0 <= need, 0.0, NEG), NEG))


def _dsa_prompt_body(rb_ref, q_ref, iq_ref, iw_ref, sz_ref, ikt_ref, kt_ref, vt_ref, tri_ref, o_ref,
                     tb_ref, sc_ref, madd_ref, q8_ref, o8_ref, qk_ref, *, topk, tq, step, widths):
    b = pl.program_id(0)
    i = pl.program_id(1)
    per_step = step // tq
    tail_w = step + tq

    @pl.when((b == 0) & (i == 0))
    def _tail_bias():
        ii = lax.broadcasted_iota(I32, (tq, tq), 0)
        jj = lax.broadcasted_iota(I32, (tq, tq), 1)
        tb_ref[...] = jnp.zeros(tb_ref.shape, F32)
        for t in range(2):
            bucket = _t5_bucket(ii - jj + tq * t)
            for h in range(N_HEADS):
                acc = jnp.zeros((tq, tq), F32)
                for kb in range(N_BUCKETS):
                    acc = jnp.where(bucket == kb, rb_ref[kb, h], acc)
                acc = acc - rb_ref[FAR_BUCKET, h]
                for r in range(per_step):
                    c = r + 1 - t
                    tb_ref[r, h, :, c * tq:(c + 1) * tq] = acc

    for h in range(N_HEADS):
        q8_ref[h] = q_ref[:, _head_rows(h)]
    pos_q = i * tq + lax.broadcasted_iota(I32, (tq, 1), 0)
    kf = jnp.minimum(topk, pos_q + 1).astype(F32)
    r_in_step = i % per_step
    tier = ((i + 1) * tq + step - 1) // step - 1

    for t, w in enumerate(widths):
        @pl.when(tier == t)
        def _(w=w):
            x_ref = sc_ref.at[:, pl.ds(0, w)]
            m_ref = madd_ref.at[:, pl.ds(0, w)]
            adm = lax.broadcasted_iota(I32, (tq, w), 1) <= pos_q
            sc = jnp.zeros((tq, w), F32)
            ikt = ikt_ref[:, :w]
            for h in range(N_HEADS):
                r = jnp.dot(iq_ref[:, h * D_IDX:(h + 1) * D_IDX], ikt, preferred_element_type=F32)
                sc = sc + jnp.maximum(r, 0.0) * iw_ref[:, h:h + 1]
            sc = sc + 0.0
            x_ref[...] = jnp.where(adm, sc, -jnp.inf)
            lo0 = jnp.min(jnp.where(adm, sc, jnp.inf), axis=-1, keepdims=True)
            hi0 = jnp.max(jnp.where(adm, sc, -jnp.inf), axis=-1, keepdims=True)
            def head_logits(h):
                qk_ref[h, :, pl.ds(0, w)] = jnp.dot(q8_ref[h], kt_ref[_head_rows(h), :w],
                                                    preferred_element_type=F32)

            thr = _kth_largest(x_ref, None, kf, lo0, hi0, side_work=head_logits)
            _topk_mask(x_ref, m_ref, tri_ref, None, kf, thr)

            tail = min(w, tail_w)

            def head(h):
                hr = _head_rows(h)
                lg = qk_ref[h, :, pl.ds(0, w)] + m_ref[...]
                near = lg[:, w - tail:] + tb_ref[r_in_step, h, :, tail_w - tail:]
                lg = near if tail == w else jnp.concatenate([lg[:, :w - tail], near], axis=1)
                p = jnp.exp(lg - jnp.max(lg, axis=-1, keepdims=True))
                den = jnp.sum(p, axis=-1, keepdims=True)
                o8_ref[h] = lax.dot_general(p.astype(BF16), vt_ref[hr, :w], NT, preferred_element_type=F32) / den

            def pair(j, carry):
                head(2 * j)
                head(2 * j + 1)
                return carry

            lax.fori_loop(0, N_HEADS // 2, pair, 0)

    for h in range(N_HEADS):
        hs = _head_rows(h)
        o_ref[:, hs] = (o8_ref[h] * sz_ref[:, hs].astype(F32)).astype(BF16)


def _dsa_prompt(pr, rel_bias, tri, batch, seq):
    tq = LANES
    assert seq % tq == 0
    nq = seq // tq
    topk = min(TOPK_MAX, seq // 4)
    step, widths = _causal_widths(seq)
    rowspec = lambda n: pl.BlockSpec((tq, n), lambda b, i: (b * nq + i, 0))
    seqspec = lambda n: pl.BlockSpec((None, n, seq), lambda b, i: (b, 0, 0))
    return pl.pallas_call(
        functools.partial(_dsa_prompt_body, topk=topk, tq=tq, step=step, widths=widths),
        grid=(batch, nq),
        in_specs=[pl.BlockSpec(memory_space=pltpu.SMEM), rowspec(HW), rowspec(HW), rowspec(N_HEADS), rowspec(HW),
                  seqspec(D_IDX), seqspec(HW), seqspec(HW), pl.BlockSpec(tri.shape, lambda b, i: (0, 0))],
        out_specs=rowspec(HW),
        out_shape=jax.ShapeDtypeStruct((batch * seq, HW), BF16),
        scratch_shapes=[pltpu.VMEM((step // tq, N_HEADS, tq, step + tq), F32), pltpu.VMEM((tq, seq), F32),
                        pltpu.VMEM((tq, seq), F32), pltpu.VMEM((N_HEADS, tq, HEAD_DIM), BF16),
                        pltpu.VMEM((N_HEADS, tq, HEAD_DIM), F32), pltpu.VMEM((N_HEADS, tq, seq), F32)],
        compiler_params=_cparams(2, VMEM_LIMIT),
        name="dsa_prompt",
    )(rel_bias, pr["qa16"], pr["iq16"], pr["iw"], pr["sza"], pr["ikt16"], pr["kat16"], pr["vat16"], tri)


def _merge_body(x_ref, ua_ref, ub_ref, sga_ref, sgb_ref, wua_ref, wub_ref, wo_ref, y_ref):
    ya = jnp.dot(ua_ref[...], wua_ref[...], preferred_element_type=F32)
    yb = jnp.dot(ub_ref[...], wub_ref[...], preferred_element_type=F32)
    m = sga_ref[...].astype(F32) * ya + sgb_ref[...].astype(F32) * yb
    y_ref[...] = x_ref[...] + jnp.dot(m.astype(BF16), wo_ref[...], preferred_element_type=F32)


def _merge(x2d, ua, ub, sga, sgb, wts):
    rows, d_model = x2d.shape
    tm = min(512, rows)
    assert rows % tm == 0
    row = lambda n: pl.BlockSpec((tm, n), lambda i: (i, 0))
    full = lambda a: pl.BlockSpec(a.shape, lambda i: (0, 0))
    return pl.pallas_call(
        _merge_body,
        grid=(rows // tm,),
        in_specs=[row(d_model), row(HW), row(HW), row(d_model), row(d_model),
                  full(wts["wua"]), full(wts["wub"]), full(wts["wo"])],
        out_specs=row(d_model),
        out_shape=jax.ShapeDtypeStruct((rows, d_model), F32),
        compiler_params=_cparams(1, VMEM_LIMIT),
        name="merge",
    )(x2d, ua, ub, sga, sgb, wts["wua"], wts["wub"], wts["wo"])


def _idx_sample_body(pt_ref, iq_ref, iw_ref, *refs):
    pages, o_ref = refs[:IDX_PAGES], refs[IDX_PAGES]
    iq = iq_ref[...]
    iw = iw_ref[...]
    rows = []
    for pg in pages:
        r = jnp.dot(iq, pg[...].astype(BF16), preferred_element_type=F32)
        rows.append(jnp.sum(jnp.maximum(r, 0.0) * iw, axis=0, keepdims=True))
    o_ref[...] = jnp.concatenate(rows, axis=0)


def _idx_sample(iq3, iw3, cache_ikt, page_table):
    bd, n_pages = page_table.shape
    assert n_pages % IDX_PAGES == 0
    ng = n_pages // IDX_PAGES
    page_specs = [pl.BlockSpec((None, D_IDX, PAGE), lambda b, g, pt, i=i: (pt[b, g * IDX_PAGES + i], 0, 0))
                  for i in range(IDX_PAGES)]
    return pl.pallas_call(
        _idx_sample_body,
        grid_spec=pltpu.PrefetchScalarGridSpec(
            num_scalar_prefetch=1, grid=(bd, ng),
            in_specs=[pl.BlockSpec((None, QPAD, D_IDX), lambda b, g, pt: (b, 0, 0)),
                      pl.BlockSpec((None, QPAD, 1), lambda b, g, pt: (b, 0, 0))] + page_specs,
            out_specs=pl.BlockSpec((None, IDX_PAGES, PAGE), lambda b, g, pt: (b, g, 0))),
        out_shape=jax.ShapeDtypeStruct((bd, n_pages, PAGE), F32),
        compiler_params=_cparams(2),
        name="idx_sample",
    )(page_table, iq3, iw3, *([cache_ikt] * IDX_PAGES))


def _select_sample_body(sc_ref, iq_ref, ik_ref, iw_ref, tri_ref, madd_ref, maddn_ref, *, topk):
    bd = sc_ref.shape[0]
    ii = lax.broadcasted_iota(I32, (bd, bd), 0)
    jj = lax.broadcasted_iota(I32, (bd, bd), 1)
    ik = ik_ref[...]
    snew = jnp.zeros((bd, 1), F32)
    for h in range(N_HEADS):
        r = lax.dot_general(iq_ref[:, h * D_IDX:(h + 1) * D_IDX], ik, NT, preferred_element_type=F32)
        rd = jnp.sum(jnp.where(ii == jj, r, 0.0), axis=-1, keepdims=True)
        snew = snew + jnp.maximum(rd, 0.0) * iw_ref[:, h:h + 1]
    snew = snew + 0.0
    x = sc_ref[...]
    kf = jnp.full((bd, 1), float(topk), F32)
    lo0 = jnp.minimum(jnp.min(x, axis=-1, keepdims=True), snew)
    hi0 = jnp.maximum(jnp.max(x, axis=-1, keepdims=True), snew)
    thr = _kth_largest(sc_ref, snew, kf, lo0, hi0)
    mn = _topk_mask(sc_ref, madd_ref, tri_ref, snew, kf, thr)
    maddn_ref[...] = jnp.broadcast_to(mn, maddn_ref.shape)


def _select_sample(scores2d, iq16, ik16, iw, tri, n_new):
    bd, past = scores2d.shape
    topk = min(TOPK_MAX, (past + n_new) // 4)
    full = lambda a: pl.BlockSpec(a.shape, lambda i: (0,) * a.ndim)
    return pl.pallas_call(
        functools.partial(_select_sample_body, topk=topk),
        grid=(1,),
        in_specs=[full(scores2d), full(iq16), full(ik16), full(iw), full(tri)],
        out_specs=[pl.BlockSpec((bd, past), lambda i: (0, 0)), pl.BlockSpec((bd, LANES), lambda i: (0, 0))],
        out_shape=[jax.ShapeDtypeStruct((bd, past), F32), jax.ShapeDtypeStruct((bd, LANES), F32)],
        compiler_params=_cparams(1, VMEM_LIMIT),
        name="select_sample",
    )(scores2d, iq16, ik16, iw, tri)


def _head_mask():
    r = lax.broadcasted_iota(I32, (QPAD, HW), 0)
    c = lax.broadcasted_iota(I32, (QPAD, HW), 1)
    return jnp.where((c >= r * HEAD_DIM) & (c < (r + 1) * HEAD_DIM), 1.0, 0.0)


def _online_update(m_ref, l_ref, acc_ref, logits, vt_pages):
    m_old = m_ref[...]
    m_new = jnp.maximum(m_old, jnp.max(logits, axis=-1, keepdims=True))
    p = jnp.exp(logits - m_new)
    corr = jnp.exp(m_old - m_new)
    l_ref[...] = l_ref[...] * corr + jnp.sum(p, axis=-1, keepdims=True)
    pv = jnp.zeros(acc_ref.shape, F32)
    for i, vp in enumerate(vt_pages):
        pv = pv + lax.dot_general(p[:, i * PAGE:(i + 1) * PAGE].astype(BF16), vp[...].astype(BF16), NT,
                                  preferred_element_type=F32)
    acc_ref[...] = acc_ref[...] * corr + pv
    m_ref[...] = m_new


def _finish_new_token(m_ref, l_ref, acc_ref, logit_new, vnew_ref, hmask, sz_ref, o_ref):
    m_old = m_ref[...]
    m_new = jnp.maximum(m_old, logit_new)
    p_new = jnp.exp(logit_new - m_new)
    corr = jnp.exp(m_old - m_new)
    den = l_ref[...] * corr + p_new
    o = (acc_ref[...] * corr + p_new * vnew_ref[...]) / den
    row = jnp.sum(o * hmask, axis=0, keepdims=True)
    o_ref[...] = (row * sz_ref[...].astype(F32)).astype(BF16)


def _dsa_sample_body(pt_ref, rbt_ref, q_ref, knew_ref, vnew_ref, sz_ref, madd_ref, maddn_ref, *refs):
    n = ATTN_PAGES
    kt_pages, vt_pages = refs[:n], refs[n:2 * n]
    o_ref, m_ref, l_ref, acc_ref, blast_ref = refs[2 * n:]
    b = pl.program_id(0)
    g = pl.program_id(1)
    last = g == pl.num_programs(1) - 1
    hmask = _head_mask()
    qbd = (q_ref[...].astype(F32) * hmask).astype(BF16)

    @pl.when((b == 0) & (g == 0))
    def _last_page_bias():
        bucket = _t5_bucket(PAGE - lax.broadcasted_iota(I32, (QPAD, PAGE), 1))
        acc = jnp.zeros((QPAD, PAGE), F32)
        for kb in range(N_BUCKETS):
            acc = jnp.where(bucket == kb, rbt_ref[:, kb:kb + 1], acc)
        blast_ref[...] = acc

    @pl.when(g == 0)
    def _init():
        m_ref[...] = jnp.full(m_ref.shape, NEG, F32)
        l_ref[...] = jnp.zeros(l_ref.shape, F32)
        acc_ref[...] = jnp.zeros(acc_ref.shape, F32)

    far = rbt_ref[:, FAR_BUCKET:FAR_BUCKET + 1]
    parts = []
    for i in range(n):
        lt = jnp.dot(qbd, kt_pages[i][...].astype(BF16), preferred_element_type=F32)
        bias = far if i < n - 1 else jnp.where(last, blast_ref[...], jnp.broadcast_to(far, (QPAD, PAGE)))
        parts.append(lt + madd_ref[i:i + 1, :] + bias)
    _online_update(m_ref, l_ref, acc_ref, jnp.concatenate(parts, axis=1), vt_pages)

    @pl.when(last)
    def _finish():
        ln = jnp.sum(qbd.astype(F32) * knew_ref[...], axis=-1, keepdims=True)
        ln = ln + rbt_ref[:, 0:1] + maddn_ref[:, 0:1]
        _finish_new_token(m_ref, l_ref, acc_ref, ln, vnew_ref, hmask, sz_ref, o_ref)


def _fox_sample_body(pt_ref, q_ref, knew_ref, vnew_ref, sz_ref, fn_ref, *refs):
    n = ATTN_PAGES
    kt_pages, vt_pages, lf_pages = refs[:n], refs[n:2 * n], refs[2 * n:3 * n]
    o_ref, m_ref, l_ref, acc_ref, c_ref = refs[3 * n:]
    g = pl.program_id(1)
    last = g == pl.num_programs(1) - 1
    hmask = _head_mask()
    qbd = (q_ref[...].astype(F32) * hmask).astype(BF16)

    @pl.when(g == 0)
    def _init():
        m_ref[...] = jnp.full(m_ref.shape, NEG, F32)
        l_ref[...] = jnp.zeros(l_ref.shape, F32)
        acc_ref[...] = jnp.zeros(acc_ref.shape, F32)
        c_ref[...] = jnp.zeros(c_ref.shape, F32)

    lf = jnp.concatenate([pg[...] for pg in lf_pages], axis=0)
    lane = lax.broadcasted_iota(I32, lf.shape, 1)
    suf = lf
    k = 1
    while k < PAGE:
        suf = suf + jnp.where(lane + k < PAGE, pltpu.roll(suf, PAGE - k, axis=1), 0.0)
        k *= 2
    later = c_ref[...]
    zpad = jnp.zeros((QPAD - N_HEADS, PAGE), F32)
    parts = []
    for i in range(n):
        rs = slice(i * N_HEADS, (i + 1) * N_HEADS)
        lt = jnp.dot(qbd, kt_pages[i][...].astype(BF16), preferred_element_type=F32)
        bias = (suf[rs] - lf[rs]) + (later + fn_ref[...])[:N_HEADS]
        parts.append(lt + jnp.concatenate([bias, zpad], axis=0))
        later = later + jnp.concatenate([suf[rs, 0:1], zpad[:, 0:1]], axis=0)
    c_ref[...] = later
    _online_update(m_ref, l_ref, acc_ref, jnp.concatenate(parts, axis=1), vt_pages)

    @pl.when(last)
    def _finish():
        ln = jnp.sum(qbd.astype(F32) * knew_ref[...], axis=-1, keepdims=True)
        _finish_new_token(m_ref, l_ref, acc_ref, ln, vnew_ref, hmask, sz_ref, o_ref)


def _row3(a):
    return a.reshape(a.shape[0], 1, a.shape[1])


def _sample_attention(body, name, page_table, small_inputs, small_specs, paged_inputs, paged_specs, bd, ng,
                      extra_scratch):
    rowspec = pl.BlockSpec((None, 1, HW), lambda b, g, pt: (b, 0, 0))
    out = pl.pallas_call(
        body,
        grid_spec=pltpu.PrefetchScalarGridSpec(
            num_scalar_prefetch=1, grid=(bd, ng),
            in_specs=small_specs + paged_specs,
            out_specs=rowspec,
            scratch_shapes=[pltpu.VMEM((QPAD, 1), F32), pltpu.VMEM((QPAD, 1), F32), pltpu.VMEM((QPAD, HW), F32),
                            extra_scratch]),
        out_shape=jax.ShapeDtypeStruct((bd, 1, HW), BF16),
        compiler_params=_cparams(2, VMEM_LIMIT),
        name=name,
    )(page_table, *small_inputs, *paged_inputs)
    return out.reshape(bd, HW)


def _dsa_sample(sp, knew, vnew, rbt, madd3, maddn, cache_kt, cache_vt, page_table):
    bd, n_pages = page_table.shape
    n = ATTN_PAGES
    assert n_pages % n == 0
    ng = n_pages // n
    rowspec = pl.BlockSpec((None, 1, HW), lambda b, g, pt: (b, 0, 0))
    page = lambda i: pl.BlockSpec((None, HW, PAGE), lambda b, g, pt, i=i: (pt[b, g * n + i], 0, 0))
    small_inputs = [rbt, _row3(sp["qa16"]), _row3(knew), _row3(vnew), _row3(sp["sza"]), madd3, _row3(maddn)]
    small_specs = [pl.BlockSpec(rbt.shape, lambda b, g, pt: (0, 0)), rowspec, rowspec, rowspec, rowspec,
                   pl.BlockSpec((None, n, PAGE), lambda b, g, pt: (b, g, 0)),
                   pl.BlockSpec((None, 1, LANES), lambda b, g, pt: (b, 0, 0))]
    return _sample_attention(_dsa_sample_body, "dsa_sample", page_table, small_inputs, small_specs,
                             [cache_kt] * n + [cache_vt] * n, [page(i) for i in range(n)] * 2, bd, ng,
                             pltpu.VMEM((QPAD, PAGE), F32))


def _fox_sample(sp, knew, vnew, fn3, cache_kt, cache_vt, cache_lft, page_table):
    bd, n_pages = page_table.shape
    n = ATTN_PAGES
    assert n_pages % n == 0
    ng = n_pages // n
    rowspec = pl.BlockSpec((None, 1, HW), lambda b, g, pt: (b, 0, 0))
    rev = lambda b, g, pt, i: pt[b, n_pages - 1 - (g * n + i)]
    page = lambda i: pl.BlockSpec((None, HW, PAGE), lambda b, g, pt, i=i: (rev(b, g, pt, i), 0, 0))
    lfpage = lambda i: pl.BlockSpec((None, N_HEADS, PAGE), lambda b, g, pt, i=i: (rev(b, g, pt, i), 0, 0))
    small_inputs = [_row3(sp["qb16"]), _row3(knew), _row3(vnew), _row3(sp["szb"]), fn3]
    small_specs = [rowspec, rowspec, rowspec, rowspec, pl.BlockSpec((None, QPAD, 1), lambda b, g, pt: (b, 0, 0))]
    return _sample_attention(_fox_sample_body, "fox_sample", page_table, small_inputs, small_specs,
                             [cache_kt] * n + [cache_vt] * n + [cache_lft] * n,
                             [page(i) for i in range(n)] * 2 + [lfpage(i) for i in range(n)], bd, ng,
                             pltpu.VMEM((QPAD, 1), F32))


def _prepare_weights(d_model, g_norm, w_in, b_fgate, qn_a, kn_a, qn_b, kn_b, w_up_a, w_up_b, w_out):
    widths = (HW, HW, HW, HW, N_HEADS * D_IDX, N_HEADS, D_IDX, HW, HW, HW, HW, N_HEADS, d_model, d_model)
    offs = [0]
    for w in widths:
        offs.append(offs[-1] + w)
    wt = w_in.T
    seg = lambda a, b: wt[offs[a]:offs[b]]
    zeros = lambda n: jnp.zeros((n, d_model), w_in.dtype)
    w_rows = jnp.concatenate([seg(0, 1), seg(3, 4), seg(4, 5), seg(7, 8), seg(10, 11), seg(12, 14),
                              seg(5, 6), zeros(LANES - N_HEADS)], axis=0)
    w_cols = jnp.concatenate([seg(1, 3), seg(8, 10), seg(6, 7), seg(11, 12),
                              zeros(LANES - D_IDX - N_HEADS)], axis=0)
    tile = lambda v: jnp.tile(v.astype(F32), N_HEADS)
    lanes = lambda v: jnp.broadcast_to(tile(v)[:, None], (HW, LANES))
    blk = jnp.arange(HW) // HEAD_DIM
    return {
        "g": g_norm.astype(F32).reshape(1, d_model),
        "wr": w_rows.astype(BF16), "wt": w_cols.astype(BF16),
        "bft": b_fgate.astype(F32).reshape(N_HEADS, 1),
        "nrm": jnp.stack([tile(qn_a), tile(qn_b)]),
        "nrmt": jnp.stack([lanes(kn_a), lanes(kn_b)]),
        "gm": jnp.where(blk[:, None] == blk[None, :], 1.0 / HEAD_DIM, 0.0).astype(BF16),
        "wua": w_up_a.astype(BF16), "wub": w_up_b.astype(BF16), "wo": w_out.astype(BF16),
    }


def _pad_rows(a, rows):
    return jnp.pad(a, ((0, 0), (0, rows - a.shape[1])) + ((0, 0),) * (a.ndim - 2))


def _pages_feature_major(cache):
    n_pool = cache.shape[0]
    flat = cache.reshape(n_pool, PAGE, -1)
    return jnp.swapaxes(flat, 1, 2)


def _heads_state(xt, lead):
    b, _, s = xt.shape
    return jnp.transpose(xt.reshape(b, N_HEADS, HEAD_DIM, s), (0, 3, 1, 2)).reshape(*lead, N_HEADS, HEAD_DIM)


def kernel(x_prompt, x_sample, cache_a_k, cache_a_v, cache_a_idx_k, cache_b_k, cache_b_v, cache_b_logf, page_table,
           rel_bias, g_norm, w_in, b_fgate, qnorm_a, knorm_a, qnorm_b, knorm_b, w_up_a, w_up_b, w_out):
    batch, seq, d_model = x_prompt.shape
    bd, t_new, _ = x_sample.shape
    depth, n_pool = cache_a_k.shape[:2]
    n_pages = page_table.shape[1]
    past = n_pages * PAGE
    assert t_new == 1, "the decode kernels handle one new token per sequence"
    assert cache_a_k.shape[2:] == (PAGE, N_HEADS, HEAD_DIM) and bd % LANES == 0

    rel_bias = rel_bias.astype(F32)
    rbt = jnp.pad(rel_bias.T, ((0, QPAD - N_HEADS), (0, 0)))
    tri = jnp.triu(jnp.ones((LANES, LANES), F32)).astype(BF16)

    xp = x_prompt.reshape(batch * seq, d_model)
    xs = x_sample.reshape(bd * t_new, d_model)
    st_p, st_s = [], []
    for layer in range(depth):
        wts = _prepare_weights(d_model, g_norm[layer], w_in[layer], b_fgate[layer], qnorm_a[layer], knorm_a[layer],
                               qnorm_b[layer], knorm_b[layer], w_up_a[layer], w_up_b[layer], w_out[layer])
        pr = _project(xp, wts, batch, seq)
        ft = _cumsum_lanes(pr["lft"], tri)
        ub = _fox_prompt(pr, ft, batch, seq)
        ua = _dsa_prompt(pr, rel_bias, tri, batch, seq)
        lead = (batch, seq)
        st_p.append((_heads_state(pr["kat"], lead), _heads_state(pr["vat"], lead), jnp.swapaxes(pr["ikt"], 1, 2),
                     _heads_state(pr["kbt"], lead), _heads_state(pr["vbt"], lead), jnp.swapaxes(pr["lft"], 1, 2)))
        xp = _merge(xp, ua, ub, pr["sga"], pr["sgb"], wts)

        sp = _project(xs, wts, 1, bd)
        rows_of = lambda name: sp[name][0].T
        iw_s, ik16_s, lf_s = sp["iw"], rows_of("ikt16"), rows_of("lft")
        iq3 = _pad_rows(sp["iq16"].reshape(bd, N_HEADS, D_IDX), QPAD)
        iw3 = _pad_rows(iw_s.reshape(bd, N_HEADS, 1), QPAD)
        scores = _idx_sample(iq3, iw3, _pages_feature_major(cache_a_idx_k[layer]), page_table)
        madd, maddn = _select_sample(scores.reshape(bd, past), sp["iq16"], ik16_s, iw_s, tri, t_new)
        ua_s = _dsa_sample(sp, rows_of("kat"), rows_of("vat"), rbt, madd.reshape(bd, n_pages, PAGE), maddn,
                           _pages_feature_major(cache_a_k[layer]), _pages_feature_major(cache_a_v[layer]), page_table)
        fn3 = _pad_rows(lf_s.reshape(bd, N_HEADS, 1), QPAD)
        ub_s = _fox_sample(sp, rows_of("kbt"), rows_of("vbt"), fn3, _pages_feature_major(cache_b_k[layer]),
                           _pages_feature_major(cache_b_v[layer]), _pages_feature_major(cache_b_logf[layer]),
                           page_table)
        lead = (bd, t_new)
        st_s.append((_heads_state(sp["kat"], lead), _heads_state(sp["vat"], lead),
                     rows_of("ikt").reshape(bd, t_new, D_IDX),
                     _heads_state(sp["kbt"], lead), _heads_state(sp["vbt"], lead), lf_s.reshape(bd, t_new, N_HEADS)))
        xs = _merge(xs, ua_s, ub_s, sp["sga"], sp["sgb"], wts)

    outs_p = [jnp.stack(z) for z in zip(*st_p)]
    outs_s = [jnp.stack(z) for z in zip(*st_s)]
    return (xp.reshape(batch, seq, d_model), xs.reshape(bd, t_new, d_model), *outs_p, *outs_s)
```

```python
import functools
import math

import jax
import jax.numpy as jnp
from jax import lax
from jax.experimental import pallas as pl
from jax.experimental.pallas import tpu as pltpu

F32 = jnp.float32
BF16 = jnp.bfloat16
I32 = jnp.int32

HEAD_DIM = 64
N_HEADS = 8
HW = N_HEADS * HEAD_DIM
D_IDX = 64
PAGE = 128
TOPK_MAX = 256
N_BUCKETS = 32
MAX_EXACT = N_BUCKETS // 2
MAX_DISTANCE = 128
FAR_BUCKET = N_BUCKETS - 1
EPS = 1e-6
NEG = -1e30
SCALE = HEAD_DIM ** -0.5
IDX_W_SCALE = (N_HEADS * D_IDX) ** -0.5

LANES = 128
ATTN_PAGES = 32
IDX_PAGES = 32
QPAD = 16
BISECT_TRIPS = N_HEADS
CAUSAL_STEP = 512
HEADS_PER_TRIP = 4
VMEM_LIMIT = 56 * 1024 * 1024

NT = (((1,), (1,)), ((), ()))


def _cparams(n_axes, vmem=None):
    return pltpu.CompilerParams(dimension_semantics=("arbitrary",) * n_axes, vmem_limit_bytes=vmem)


def _log_sigmoid(z):
    return -(jnp.maximum(-z, 0.0) + jnp.log1p(jnp.exp(-jnp.abs(z))))


def _t5_bucket(dist):
    d = jnp.maximum(dist, 0)
    df = jnp.maximum(d, 1).astype(F32)
    large = MAX_EXACT + (jnp.log(df / MAX_EXACT) / math.log(MAX_DISTANCE / MAX_EXACT)
                         * (N_BUCKETS - MAX_EXACT)).astype(I32)
    large = jnp.minimum(large, N_BUCKETS - 1)
    return jnp.where(d < MAX_EXACT, d, large)


def _head_rows(h):
    if isinstance(h, int):
        return slice(h * HEAD_DIM, (h + 1) * HEAD_DIM)
    return pl.ds(pl.multiple_of(h * HEAD_DIM, HEAD_DIM), HEAD_DIM)


def _causal_widths(seq):
    step = CAUSAL_STEP if seq % CAUSAL_STEP == 0 else seq
    return step, [step * (t + 1) for t in range(seq // step)]


def _project_body(x_ref, g_ref, wr_ref, wt_ref, bft_ref, nrm_ref, nrmt_ref, gm_ref,
                  qa_o, sza_o, iq_o, iw_o, qb_o, szb_o, sga_o, sgb_o,
                  kat_o, vat_o, kbt_o, vbt_o, kat16_o, vat16_o, kbt16_o, vbt16_o, ikt_o, ikt16_o, lft_o,
                  *, d_model, tm):
    x = x_ref[...]
    ms = jnp.mean(x * x, axis=-1, keepdims=True)
    h = ((x * lax.rsqrt(ms + EPS)) * g_ref[...]).astype(BF16)
    gm = gm_ref[...]
    nd = d_model // HW

    def rows(i, n=HW):
        return lax.dot_general(h, wr_ref[i * HW:i * HW + n, :], NT, preferred_element_type=F32)

    def cols(i, n=HW):
        return lax.dot_general(wt_ref[i * HW:i * HW + n, :], h, NT, preferred_element_type=F32)

    def head_norm(p, row):
        ss = jnp.dot((p * p).astype(BF16), gm, preferred_element_type=F32)
        return (p * lax.rsqrt(ss + EPS)) * nrm_ref[row:row + 1, :]

    def head_norm_t(pt, j):
        ss = jnp.dot(gm, (pt * pt).astype(BF16), preferred_element_type=F32)
        return (pt * lax.rsqrt(ss + EPS)) * jnp.concatenate([nrmt_ref[j]] * (tm // LANES), axis=1)

    def silu(z):
        return z * jax.nn.sigmoid(z)

    qa_o[...] = (head_norm(rows(0), 0) * SCALE).astype(BF16)
    sza_o[...] = silu(rows(1)).astype(BF16)
    iq_o[...] = rows(2).astype(BF16)
    qb_o[...] = (head_norm(rows(3), 1) * SCALE).astype(BF16)
    szb_o[...] = silu(rows(4)).astype(BF16)
    for c in range(nd):
        sga_o[:, c * HW:(c + 1) * HW] = jax.nn.sigmoid(rows(5 + c)).astype(BF16)
        sgb_o[:, c * HW:(c + 1) * HW] = jax.nn.sigmoid(rows(5 + nd + c)).astype(BF16)
    iw_o[...] = rows(5 + 2 * nd, LANES)[:, 0:N_HEADS] * IDX_W_SCALE

    kat = head_norm_t(cols(0), 0)
    kat_o[...] = kat
    kat16_o[...] = kat.astype(BF16)
    vat = cols(1)
    vat_o[...] = vat
    vat16_o[...] = vat.astype(BF16)
    kbt = head_norm_t(cols(2), 1)
    kbt_o[...] = kbt
    kbt16_o[...] = kbt.astype(BF16)
    vbt = cols(3)
    vbt_o[...] = vbt
    vbt16_o[...] = vbt.astype(BF16)
    small = cols(4, LANES)
    ikt = small[0:D_IDX]
    ikt_o[...] = ikt
    ikt16_o[...] = ikt.astype(BF16)
    lft_o[...] = _log_sigmoid(small[D_IDX:D_IDX + N_HEADS] + bft_ref[...])


def _project(x2d, wts, batch, seq):
    rows, d_model = x2d.shape
    assert rows == batch * seq
    tm = min(512, seq)
    assert seq % tm == 0 and tm % LANES == 0 and d_model % HW == 0
    nt = seq // tm
    consts = (wts["g"], wts["wr"], wts["wt"], wts["bft"], wts["nrm"], wts["nrmt"], wts["gm"])

    def row(n, dt):
        return jax.ShapeDtypeStruct((rows, n), dt), pl.BlockSpec((tm, n), lambda b, i: (b * nt + i, 0))

    def col(n, dt):
        return (jax.ShapeDtypeStruct((batch, n, seq), dt), pl.BlockSpec((None, n, tm), lambda b, i: (b, 0, i)))

    outs = [row(HW, BF16), row(HW, BF16), row(HW, BF16), row(N_HEADS, F32), row(HW, BF16), row(HW, BF16),
            row(d_model, BF16), row(d_model, BF16),
            col(HW, F32), col(HW, F32), col(HW, F32), col(HW, F32),
            col(HW, BF16), col(HW, BF16), col(HW, BF16), col(HW, BF16),
            col(D_IDX, F32), col(D_IDX, BF16), col(N_HEADS, F32)]
    names = ("qa16", "sza", "iq16", "iw", "qb16", "szb", "sga", "sgb",
             "kat", "vat", "kbt", "vbt", "kat16", "vat16", "kbt16", "vbt16", "ikt", "ikt16", "lft")
    res = pl.pallas_call(
        functools.partial(_project_body, d_model=d_model, tm=tm),
        grid=(batch, nt),
        in_specs=[pl.BlockSpec((tm, d_model), lambda b, i: (b * nt + i, 0))]
        + [pl.BlockSpec(c.shape, lambda b, i, n=c.ndim: (0,) * n, pipeline_mode=pl.Buffered(1)) for c in consts],
        out_specs=[o[1] for o in outs],
        out_shape=[o[0] for o in outs],
        compiler_params=_cparams(2, VMEM_LIMIT),
        name="project",
    )(x2d, *consts)
    return dict(zip(names, res))


def _split3(a):
    hi = a.astype(BF16)
    r1 = a - hi.astype(F32)
    mid = r1.astype(BF16)
    lo = (r1 - mid.astype(F32)).astype(BF16)
    return hi, mid, lo


def _cumsum_body(x_ref, tri_ref, o_ref, *, seq):
    tri = tri_ref[...]
    carry = jnp.zeros((x_ref.shape[0], 1), F32)
    for c in range(seq // LANES):
        hi, mid, lo = _split3(x_ref[:, c * LANES:(c + 1) * LANES])
        cs = (jnp.dot(hi, tri, preferred_element_type=F32) + jnp.dot(mid, tri, preferred_element_type=F32)
              + jnp.dot(lo, tri, preferred_element_type=F32)) + carry
        o_ref[:, c * LANES:(c + 1) * LANES] = cs
        carry = cs[:, LANES - 1:LANES]


def _cumsum_lanes(xt, tri):
    batch, n, seq = xt.shape
    spec = pl.BlockSpec((None, n, seq), lambda b: (b, 0, 0))
    return pl.pallas_call(
        functools.partial(_cumsum_body, seq=seq),
        grid=(batch,),
        in_specs=[spec, pl.BlockSpec(tri.shape, lambda b: (0, 0))],
        out_specs=spec,
        out_shape=jax.ShapeDtypeStruct(xt.shape, F32),
        compiler_params=_cparams(1),
        name="cumsum",
    )(xt, tri)


def _fox_prompt_body(q_ref, kt_ref, vt_ref, ft_ref, sz_ref, o_ref, q8_ref, o8_ref, *, tq, step, widths):
    i = pl.program_id(1)
    for h in range(N_HEADS):
        q8_ref[h] = q_ref[:, _head_rows(h)]
    tier = ((i + 1) * tq + step - 1) // step - 1

    for t, w in enumerate(widths):
        @pl.when(tier == t)
        def _(w=w):
            row_pos = i * tq + lax.broadcasted_iota(I32, (tq, step), 0)
            col_pos = (w - step) + lax.broadcasted_iota(I32, (tq, step), 1)
            visible = col_pos <= row_pos

            def head(h):
                hr = _head_rows(h)
                s = jnp.dot(q8_ref[h], kt_ref[hr, :w], preferred_element_type=F32) - ft_ref[pl.ds(h, 1), :w]
                tail = jnp.where(visible, s[:, w - step:], NEG)
                s = tail if w == step else jnp.concatenate([s[:, :w - step], tail], axis=1)
                p = jnp.exp(s - jnp.max(s, axis=-1, keepdims=True))
                den = jnp.sum(p, axis=-1, keepdims=True)
                o8_ref[h] = lax.dot_general(p.astype(BF16), vt_ref[hr, :w], NT, preferred_element_type=F32) / den

            def group(j, carry):
                for u in range(HEADS_PER_TRIP):
                    head(HEADS_PER_TRIP * j + u)
                return carry

            lax.fori_loop(0, N_HEADS // HEADS_PER_TRIP, group, 0)

    for h in range(N_HEADS):
        hs = _head_rows(h)
        o_ref[:, hs] = (o8_ref[h] * sz_ref[:, hs].astype(F32)).astype(BF16)


def _fox_prompt(pr, ft, batch, seq):
    tq = min(256, seq)
    nq = seq // tq
    step, widths = _causal_widths(seq)
    assert step % tq == 0
    rowspec = pl.BlockSpec((tq, HW), lambda b, i: (b * nq + i, 0))
    seqspec = lambda n: pl.BlockSpec((None, n, seq), lambda b, i: (b, 0, 0))
    return pl.pallas_call(
        functools.partial(_fox_prompt_body, tq=tq, step=step, widths=widths),
        grid=(batch, nq),
        in_specs=[rowspec, seqspec(HW), seqspec(HW), seqspec(N_HEADS), rowspec],
        out_specs=rowspec,
        out_shape=jax.ShapeDtypeStruct((batch * seq, HW), BF16),
        scratch_shapes=[pltpu.VMEM((N_HEADS, tq, HEAD_DIM), BF16), pltpu.VMEM((N_HEADS, tq, HEAD_DIM), F32)],
        compiler_params=_cparams(2, VMEM_LIMIT),
        name="fox_prompt",
    )(pr["qb16"], pr["kbt16"], pr["vbt16"], ft, pr["szb"])


def _count(pred):
    return jnp.sum(jnp.where(pred, 1.0, 0.0), axis=-1, keepdims=True)


def _kth_largest(x_ref, extra, kf, lo0, hi0, side_work=None):
    def cnt_gt(v):
        c = _count(x_ref[...] > v)
        return c if extra is None else c + jnp.where(extra > v, 1.0, 0.0)

    def cnt_ge(v):
        c = _count(x_ref[...] >= v)
        return c if extra is None else c + jnp.where(extra >= v, 1.0, 0.0)

    def max_where(pred_fn):
        x = x_ref[...]
        m = jnp.max(jnp.where(pred_fn(x), x, -jnp.inf), axis=-1, keepdims=True)
        return m if extra is None else jnp.maximum(m, jnp.where(pred_fn(extra), extra, -jnp.inf))

    def bisect(_, c):
        lo, hi = c
        mid = 0.5 * (lo + hi)
        above = cnt_gt(mid) >= kf
        return jnp.where(above, mid, lo), jnp.where(above, hi, mid)

    def trip(j, c):
        if side_work is not None:
            side_work(j)
        return bisect(j, bisect(j, c))

    lo, hi = lax.fori_loop(0, BISECT_TRIPS, trip, (lo0, hi0))

    def finish(c):
        lo, hi, _ = c
        lo, hi = lax.fori_loop(0, 4, bisect, (lo, hi), unroll=True)
        t1 = max_where(lambda x: x <= hi)
        ok = cnt_ge(t1) >= kf
        t2 = max_where(lambda x: x < t1)
        pending = jnp.sum(jnp.where(ok, 0.0, 1.0))
        return jnp.where(ok, t1, lo), jnp.where(ok, t1, t2), pending

    _, hi, _ = lax.while_loop(lambda c: c[2] > 0.0, finish, (lo, hi, jnp.float32(1.0)))
    return hi


def _topk_mask(x_ref, madd_ref, tri_ref, extra, kf, thr):
    x = x_ref[...]
    c_gt = _count(x > thr)
    c_ge = _count(x >= thr)
    if extra is not None:
        c_gt = c_gt + jnp.where(extra > thr, 1.0, 0.0)
        c_ge = c_ge + jnp.where(extra >= thr, 1.0, 0.0)
    need = kf - c_gt
    madd_ref[...] = jnp.where(x >= thr, 0.0, NEG)
    surplus = jnp.sum(jnp.where(c_ge > kf, 1.0, 0.0))

    @pl.when(surplus > 0.0)
    def _():
        tri = tri_ref[...]
        seen = jnp.zeros_like(thr)
        for c in range(x_ref.shape[1] // LANES):
            cs = slice(c * LANES, (c + 1) * LANES)
            xc = x_ref[:, cs]
            tie = jnp.where(xc == thr, 1.0, 0.0)
            rank = jnp.dot(tie.astype(BF16), tri, preferred_element_type=F32) + seen
            madd_ref[:, cs] = jnp.where(xc > thr, 0.0, jnp.where(xc == thr, jnp.where(rank <= need, 0.0, NEG), NEG))
            seen = rank[:, LANES - 1:LANES]

    if extra is None:
        return None
    ties_before = c_ge - c_gt - jnp.where(extra == thr, 1.0, 0.0)
    return jnp.where(extra > thr, 0.0, jnp.where(extra == thr, jnp.where(ties_before + 1.0 <= need, 0.0, NEG), NEG))


def _dsa_prompt_body(rb_ref, q_ref, iq_ref, iw_ref, sz_ref, ikt_ref, kt_ref, vt_ref, tri_ref, o_ref,
                     tb_ref, sc_ref, madd_ref, q8_ref, o8_ref, qk_ref, *, topk, tq, step, widths):
    b = pl.program_id(0)
    i = pl.program_id(1)
    per_step = step // tq
    tail_w = step + tq

    @pl.when((b == 0) & (i == 0))
    def _tail_bias():
        ii = lax.broadcasted_iota(I32, (tq, tq), 0)
        jj = lax.broadcasted_iota(I32, (tq, tq), 1)
        tb_ref[...] = jnp.zeros(tb_ref.shape, F32)
        for t in range(2):
            bucket = _t5_bucket(ii - jj + tq * t)
            for h in range(N_HEADS):
                acc = jnp.zeros((tq, tq), F32)
                for kb in range(N_BUCKETS):
                    acc = jnp.where(bucket == kb, rb_ref[kb, h], acc)
                acc = acc - rb_ref[FAR_BUCKET, h]
                for r in range(per_step):
                    c = r + 1 - t
                    tb_ref[r, h, :, c * tq:(c + 1) * tq] = acc

    for h in range(N_HEADS):
        q8_ref[h] = q_ref[:, _head_rows(h)]
    pos_q = i * tq + lax.broadcasted_iota(I32, (tq, 1), 0)
    kf = jnp.minimum(topk, pos_q + 1).astype(F32)
    r_in_step = i % per_step
    tier = ((i + 1) * tq + step - 1) // step - 1

    for t, w in enumerate(widths):
        @pl.when(tier == t)
        def _(w=w):
            x_ref = sc_ref.at[:, pl.ds(0, w)]
            m_ref = madd_ref.at[:, pl.ds(0, w)]
            adm = lax.broadcasted_iota(I32, (tq, w), 1) <= pos_q
            sc = jnp.zeros((tq, w), F32)
            ikt = ikt_ref[:, :w]
            for h in range(N_HEADS):
                r = jnp.dot(iq_ref[:, h * D_IDX:(h + 1) * D_IDX], ikt, preferred_element_type=F32)
                sc = sc + jnp.maximum(r, 0.0) * iw_ref[:, h:h + 1]
            sc = sc + 0.0
            x_ref[...] = jnp.where(adm, sc, -jnp.inf)
            lo0 = jnp.min(jnp.where(adm, sc, jnp.inf), axis=-1, keepdims=True)
            hi0 = jnp.max(jnp.where(adm, sc, -jnp.inf), axis=-1, keepdims=True)
            def head_logits(h):
                qk_ref[h, :, pl.ds(0, w)] = jnp.dot(q8_ref[h], kt_ref[_head_rows(h), :w],
                                                    preferred_element_type=F32)

            thr = _kth_largest(x_ref, None, kf, lo0, hi0, side_work=head_logits)
            _topk_mask(x_ref, m_ref, tri_ref, None, kf, thr)

            tail = min(w, tail_w)

            def head(h):
                hr = _head_rows(h)
                lg = qk_ref[h, :, pl.ds(0, w)] + m_ref[...]
                near = lg[:, w - tail:] + tb_ref[r_in_step, h, :, tail_w - tail:]
                lg = near if tail == w else jnp.concatenate([lg[:, :w - tail], near], axis=1)
                p = jnp.exp(lg - jnp.max(lg, axis=-1, keepdims=True))
                den = jnp.sum(p, axis=-1, keepdims=True)
                o8_ref[h] = lax.dot_general(p.astype(BF16), vt_ref[hr, :w], NT, preferred_element_type=F32) / den

            def pair(j, carry):
                head(2 * j)
                head(2 * j + 1)
                return carry

            lax.fori_loop(0, N_HEADS // 2, pair, 0)

    for h in range(N_HEADS):
        hs = _head_rows(h)
        o_ref[:, hs] = (o8_ref[h] * sz_ref[:, hs].astype(F32)).astype(BF16)


def _dsa_prompt(pr, rel_bias, tri, batch, seq):
    tq = LANES
    assert seq % tq == 0
    nq = seq // tq
    topk = min(TOPK_MAX, seq // 4)
    step, widths = _causal_widths(seq)
    rowspec = lambda n: pl.BlockSpec((tq, n), lambda b, i: (b * nq + i, 0))
    seqspec = lambda n: pl.BlockSpec((None, n, seq), lambda b, i: (b, 0, 0))
    return pl.pallas_call(
        functools.partial(_dsa_prompt_body, topk=topk, tq=tq, step=step, widths=widths),
        grid=(batch, nq),
        in_specs=[pl.BlockSpec(memory_space=pltpu.SMEM), rowspec(HW), rowspec(HW), rowspec(N_HEADS), rowspec(HW),
                  seqspec(D_IDX), seqspec(HW), seqspec(HW), pl.BlockSpec(tri.shape, lambda b, i: (0, 0))],
        out_specs=rowspec(HW),
        out_shape=jax.ShapeDtypeStruct((batch * seq, HW), BF16),
        scratch_shapes=[pltpu.VMEM((step // tq, N_HEADS, tq, step + tq), F32), pltpu.VMEM((tq, seq), F32),
                        pltpu.VMEM((tq, seq), F32), pltpu.VMEM((N_HEADS, tq, HEAD_DIM), BF16),
                        pltpu.VMEM((N_HEADS, tq, HEAD_DIM), F32), pltpu.VMEM((N_HEADS, tq, seq), F32)],
        compiler_params=_cparams(2, VMEM_LIMIT),
        name="dsa_prompt",
    )(rel_bias, pr["qa16"], pr["iq16"], pr["iw"], pr["sza"], pr["ikt16"], pr["kat16"], pr["vat16"], tri)


def _merge_body(x_ref, ua_ref, ub_ref, sga_ref, sgb_ref, wua_ref, wub_ref, wo_ref, y_ref):
    ya = jnp.dot(ua_ref[...], wua_ref[...], preferred_element_type=F32)
    yb = jnp.dot(ub_ref[...], wub_ref[...], preferred_element_type=F32)
    m = sga_ref[...].astype(F32) * ya + sgb_ref[...].astype(F32) * yb
    y_ref[...] = x_ref[...] + jnp.dot(m.astype(BF16), wo_ref[...], preferred_element_type=F32)


def _merge(x2d, ua, ub, sga, sgb, wts):
    rows, d_model = x2d.shape
    tm = min(512, rows)
    assert rows % tm == 0
    row = lambda n: pl.BlockSpec((tm, n), lambda i: (i, 0))
    full = lambda a: pl.BlockSpec(a.shape, lambda i: (0, 0))
    return pl.pallas_call(
        _merge_body,
        grid=(rows // tm,),
        in_specs=[row(d_model), row(HW), row(HW), row(d_model), row(d_model),
                  full(wts["wua"]), full(wts["wub"]), full(wts["wo"])],
        out_specs=row(d_model),
        out_shape=jax.ShapeDtypeStruct((rows, d_model), F32),
        compiler_params=_cparams(1, VMEM_LIMIT),
        name="merge",
    )(x2d, ua, ub, sga, sgb, wts["wua"], wts["wub"], wts["wo"])


def _idx_sample_body(pt_ref, iq_ref, iw_ref, *refs):
    pages, o_ref = refs[:IDX_PAGES], refs[IDX_PAGES]
    keys = jnp.concatenate([pg[...].astype(BF16) for pg in pages], axis=1)
    r = jnp.dot(iq_ref[...], keys, preferred_element_type=F32)
    s = jnp.sum(jnp.maximum(r, 0.0) * iw_ref[...], axis=0, keepdims=True)
    o_ref[...] = jnp.concatenate([s[:, i * PAGE:(i + 1) * PAGE] for i in range(IDX_PAGES)], axis=0)


def _idx_sample(iq3, iw3, cache_ikt, page_table):
    bd, n_pages = page_table.shape
    assert n_pages % IDX_PAGES == 0
    ng = n_pages // IDX_PAGES
    page_specs = [pl.BlockSpec((None, D_IDX, PAGE), lambda b, g, pt, i=i: (pt[b, g * IDX_PAGES + i], 0, 0))
                  for i in range(IDX_PAGES)]
    return pl.pallas_call(
        _idx_sample_body,
        grid_spec=pltpu.PrefetchScalarGridSpec(
            num_scalar_prefetch=1, grid=(bd, ng),
            in_specs=[pl.BlockSpec((None, QPAD, D_IDX), lambda b, g, pt: (b, 0, 0)),
                      pl.BlockSpec((None, QPAD, 1), lambda b, g, pt: (b, 0, 0))] + page_specs,
            out_specs=pl.BlockSpec((None, IDX_PAGES, PAGE), lambda b, g, pt: (b, g, 0))),
        out_shape=jax.ShapeDtypeStruct((bd, n_pages, PAGE), F32),
        compiler_params=_cparams(2),
        name="idx_sample",
    )(page_table, iq3, iw3, *([cache_ikt] * IDX_PAGES))


def _select_sample_body(sc_ref, iq_ref, ik_ref, iw_ref, tri_ref, madd_ref, maddn_ref, *, topk):
    bd = sc_ref.shape[0]
    ii = lax.broadcasted_iota(I32, (bd, bd), 0)
    jj = lax.broadcasted_iota(I32, (bd, bd), 1)
    ik = ik_ref[...]
    snew = jnp.zeros((bd, 1), F32)
    for h in range(N_HEADS):
        r = lax.dot_general(iq_ref[:, h * D_IDX:(h + 1) * D_IDX], ik, NT, preferred_element_type=F32)
        rd = jnp.sum(jnp.where(ii == jj, r, 0.0), axis=-1, keepdims=True)
        snew = snew + jnp.maximum(rd, 0.0) * iw_ref[:, h:h + 1]
    snew = snew + 0.0
    x = sc_ref[...]
    kf = jnp.full((bd, 1), float(topk), F32)
    lo0 = jnp.minimum(jnp.min(x, axis=-1, keepdims=True), snew)
    hi0 = jnp.maximum(jnp.max(x, axis=-1, keepdims=True), snew)
    thr = _kth_largest(sc_ref, snew, kf, lo0, hi0)
    mn = _topk_mask(sc_ref, madd_ref, tri_ref, snew, kf, thr)
    maddn_ref[...] = jnp.broadcast_to(mn, maddn_ref.shape)


def _select_sample(scores2d, iq16, ik16, iw, tri, n_new):
    bd, past = scores2d.shape
    topk = min(TOPK_MAX, (past + n_new) // 4)
    full = lambda a: pl.BlockSpec(a.shape, lambda i: (0,) * a.ndim)
    return pl.pallas_call(
        functools.partial(_select_sample_body, topk=topk),
        grid=(1,),
        in_specs=[full(scores2d), full(iq16), full(ik16), full(iw), full(tri)],
        out_specs=[pl.BlockSpec((bd, past), lambda i: (0, 0)), pl.BlockSpec((bd, LANES), lambda i: (0, 0))],
        out_shape=[jax.ShapeDtypeStruct((bd, past), F32), jax.ShapeDtypeStruct((bd, LANES), F32)],
        compiler_params=_cparams(1, VMEM_LIMIT),
        name="select_sample",
    )(scores2d, iq16, ik16, iw, tri)


def _head_mask():
    r = lax.broadcasted_iota(I32, (QPAD, HW), 0)
    c = lax.broadcasted_iota(I32, (QPAD, HW), 1)
    return jnp.where((c >= r * HEAD_DIM) & (c < (r + 1) * HEAD_DIM), 1.0, 0.0)


def _online_update(m_ref, l_ref, acc_ref, logits, vt_pages):
    m_old = m_ref[...]
    m_new = jnp.maximum(m_old, jnp.max(logits, axis=-1, keepdims=True))
    p = jnp.exp(logits - m_new)
    corr = jnp.exp(m_old - m_new)
    l_ref[...] = l_ref[...] * corr + jnp.sum(p, axis=-1, keepdims=True)
    pv = jnp.zeros(acc_ref.shape, F32)
    for i, vp in enumerate(vt_pages):
        pv = pv + lax.dot_general(p[:, i * PAGE:(i + 1) * PAGE].astype(BF16), vp[...].astype(BF16), NT,
                                  preferred_element_type=F32)
    acc_ref[...] = acc_ref[...] * corr + pv
    m_ref[...] = m_new


def _finish_new_token(m_ref, l_ref, acc_ref, logit_new, vnew_ref, hmask, sz_ref, o_ref):
    m_old = m_ref[...]
    m_new = jnp.maximum(m_old, logit_new)
    p_new = jnp.exp(logit_new - m_new)
    corr = jnp.exp(m_old - m_new)
    den = l_ref[...] * corr + p_new
    o = (acc_ref[...] * corr + p_new * vnew_ref[...]) / den
    row = jnp.sum(o * hmask, axis=0, keepdims=True)
    o_ref[...] = (row * sz_ref[...].astype(F32)).astype(BF16)


def _dsa_sample_body(pt_ref, rbt_ref, q_ref, knew_ref, vnew_ref, sz_ref, madd_ref, maddn_ref, *refs):
    n = ATTN_PAGES
    kt_pages, vt_pages = refs[:n], refs[n:2 * n]
    o_ref, m_ref, l_ref, acc_ref, blast_ref = refs[2 * n:]
    b = pl.program_id(0)
    g = pl.program_id(1)
    last = g == pl.num_programs(1) - 1
    hmask = _head_mask()
    qbd = (q_ref[...].astype(F32) * hmask).astype(BF16)

    @pl.when((b == 0) & (g == 0))
    def _last_page_bias():
        bucket = _t5_bucket(PAGE - lax.broadcasted_iota(I32, (QPAD, PAGE), 1))
        acc = jnp.zeros((QPAD, PAGE), F32)
        for kb in range(N_BUCKETS):
            acc = jnp.where(bucket == kb, rbt_ref[:, kb:kb + 1], acc)
        blast_ref[...] = acc

    @pl.when(g == 0)
    def _init():
        m_ref[...] = jnp.full(m_ref.shape, NEG, F32)
        l_ref[...] = jnp.zeros(l_ref.shape, F32)
        acc_ref[...] = jnp.zeros(acc_ref.shape, F32)

    far = rbt_ref[:, FAR_BUCKET:FAR_BUCKET + 1]
    parts = []
    for i in range(n):
        lt = jnp.dot(qbd, kt_pages[i][...].astype(BF16), preferred_element_type=F32)
        bias = far if i < n - 1 else jnp.where(last, blast_ref[...], jnp.broadcast_to(far, (QPAD, PAGE)))
        parts.append(lt + madd_ref[i:i + 1, :] + bias)
    _online_update(m_ref, l_ref, acc_ref, jnp.concatenate(parts, axis=1), vt_pages)

    @pl.when(last)
    def _finish():
        ln = jnp.sum(qbd.astype(F32) * knew_ref[...], axis=-1, keepdims=True)
        ln = ln + rbt_ref[:, 0:1] + maddn_ref[:, 0:1]
        _finish_new_token(m_ref, l_ref, acc_ref, ln, vnew_ref, hmask, sz_ref, o_ref)


def _fox_sample_body(pt_ref, q_ref, knew_ref, vnew_ref, sz_ref, fn_ref, *refs):
    n = ATTN_PAGES
    kt_pages, vt_pages, lf_pages = refs[:n], refs[n:2 * n], refs[2 * n:3 * n]
    o_ref, m_ref, l_ref, acc_ref, c_ref = refs[3 * n:]
    g = pl.program_id(1)
    last = g == pl.num_programs(1) - 1
    hmask = _head_mask()
    qbd = (q_ref[...].astype(F32) * hmask).astype(BF16)

    @pl.when(g == 0)
    def _init():
        m_ref[...] = jnp.full(m_ref.shape, NEG, F32)
        l_ref[...] = jnp.zeros(l_ref.shape, F32)
        acc_ref[...] = jnp.zeros(acc_ref.shape, F32)
        c_ref[...] = jnp.zeros(c_ref.shape, F32)

    lf = jnp.concatenate([pg[...] for pg in lf_pages], axis=0)
    lane = lax.broadcasted_iota(I32, lf.shape, 1)
    suf = lf
    k = 1
    while k < PAGE:
        suf = suf + jnp.where(lane + k < PAGE, pltpu.roll(suf, PAGE - k, axis=1), 0.0)
        k *= 2
    later = c_ref[...]
    zpad = jnp.zeros((QPAD - N_HEADS, PAGE), F32)
    parts = []
    for i in range(n):
        rs = slice(i * N_HEADS, (i + 1) * N_HEADS)
        lt = jnp.dot(qbd, kt_pages[i][...].astype(BF16), preferred_element_type=F32)
        bias = (suf[rs] - lf[rs]) + (later + fn_ref[...])[:N_HEADS]
        parts.append(lt + jnp.concatenate([bias, zpad], axis=0))
        later = later + jnp.concatenate([suf[rs, 0:1], zpad[:, 0:1]], axis=0)
    c_ref[...] = later
    _online_update(m_ref, l_ref, acc_ref, jnp.concatenate(parts, axis=1), vt_pages)

    @pl.when(last)
    def _finish():
        ln = jnp.sum(qbd.astype(F32) * knew_ref[...], axis=-1, keepdims=True)
        _finish_new_token(m_ref, l_ref, acc_ref, ln, vnew_ref, hmask, sz_ref, o_ref)


def _row3(a):
    return a.reshape(a.shape[0], 1, a.shape[1])


def _sample_attention(body, name, page_table, small_inputs, small_specs, paged_inputs, paged_specs, bd, ng,
                      extra_scratch):
    rowspec = pl.BlockSpec((None, 1, HW), lambda b, g, pt: (b, 0, 0))
    out = pl.pallas_call(
        body,
        grid_spec=pltpu.PrefetchScalarGridSpec(
            num_scalar_prefetch=1, grid=(bd, ng),
            in_specs=small_specs + paged_specs,
            out_specs=rowspec,
            scratch_shapes=[pltpu.VMEM((QPAD, 1), F32), pltpu.VMEM((QPAD, 1), F32), pltpu.VMEM((QPAD, HW), F32),
                            extra_scratch]),
        out_shape=jax.ShapeDtypeStruct((bd, 1, HW), BF16),
        compiler_params=_cparams(2, VMEM_LIMIT),
        name=name,
    )(page_table, *small_inputs, *paged_inputs)
    return out.reshape(bd, HW)


def _dsa_sample(sp, knew, vnew, rbt, madd3, maddn, cache_kt, cache_vt, page_table):
    bd, n_pages = page_table.shape
    n = ATTN_PAGES
    assert n_pages % n == 0
    ng = n_pages // n
    rowspec = pl.BlockSpec((None, 1, HW), lambda b, g, pt: (b, 0, 0))
    page = lambda i: pl.BlockSpec((None, HW, PAGE), lambda b, g, pt, i=i: (pt[b, g * n + i], 0, 0))
    small_inputs = [rbt, _row3(sp["qa16"]), _row3(knew), _row3(vnew), _row3(sp["sza"]), madd3, _row3(maddn)]
    small_specs = [pl.BlockSpec(rbt.shape, lambda b, g, pt: (0, 0)), rowspec, rowspec, rowspec, rowspec,
                   pl.BlockSpec((None, n, PAGE), lambda b, g, pt: (b, g, 0)),
                   pl.BlockSpec((None, 1, LANES), lambda b, g, pt: (b, 0, 0))]
    return _sample_attention(_dsa_sample_body, "dsa_sample", page_table, small_inputs, small_specs,
                             [cache_kt] * n + [cache_vt] * n, [page(i) for i in range(n)] * 2, bd, ng,
                             pltpu.VMEM((QPAD, PAGE), F32))


def _fox_sample(sp, knew, vnew, fn3, cache_kt, cache_vt, cache_lft, page_table):
    bd, n_pages = page_table.shape
    n = ATTN_PAGES
    assert n_pages % n == 0
    ng = n_pages // n
    rowspec = pl.BlockSpec((None, 1, HW), lambda b, g, pt: (b, 0, 0))
    rev = lambda b, g, pt, i: pt[b, n_pages - 1 - (g * n + i)]
    page = lambda i: pl.BlockSpec((None, HW, PAGE), lambda b, g, pt, i=i: (rev(b, g, pt, i), 0, 0))
    lfpage = lambda i: pl.BlockSpec((None, N_HEADS, PAGE), lambda b, g, pt, i=i: (rev(b, g, pt, i), 0, 0))
    small_inputs = [_row3(sp["qb16"]), _row3(knew), _row3(vnew), _row3(sp["szb"]), fn3]
    small_specs = [rowspec, rowspec, rowspec, rowspec, pl.BlockSpec((None, QPAD, 1), lambda b, g, pt: (b, 0, 0))]
    return _sample_attention(_fox_sample_body, "fox_sample", page_table, small_inputs, small_specs,
                             [cache_kt] * n + [cache_vt] * n + [cache_lft] * n,
                             [page(i) for i in range(n)] * 2 + [lfpage(i) for i in range(n)], bd, ng,
                             pltpu.VMEM((QPAD, 1), F32))


def _prepare_weights(d_model, g_norm, w_in, b_fgate, qn_a, kn_a, qn_b, kn_b, w_up_a, w_up_b, w_out):
    widths = (HW, HW, HW, HW, N_HEADS * D_IDX, N_HEADS, D_IDX, HW, HW, HW, HW, N_HEADS, d_model, d_model)
    offs = [0]
    for w in widths:
        offs.append(offs[-1] + w)
    wt = w_in.T
    seg = lambda a, b: wt[offs[a]:offs[b]]
    zeros = lambda n: jnp.zeros((n, d_model), w_in.dtype)
    w_rows = jnp.concatenate([seg(0, 1), seg(3, 4), seg(4, 5), seg(7, 8), seg(10, 11), seg(12, 14),
                              seg(5, 6), zeros(LANES - N_HEADS)], axis=0)
    w_cols = jnp.concatenate([seg(1, 3), seg(8, 10), seg(6, 7), seg(11, 12),
                              zeros(LANES - D_IDX - N_HEADS)], axis=0)
    tile = lambda v: jnp.tile(v.astype(F32), N_HEADS)
    lanes = lambda v: jnp.broadcast_to(tile(v)[:, None], (HW, LANES))
    blk = jnp.arange(HW) // HEAD_DIM
    return {
        "g": g_norm.astype(F32).reshape(1, d_model),
        "wr": w_rows.astype(BF16), "wt": w_cols.astype(BF16),
        "bft": b_fgate.astype(F32).reshape(N_HEADS, 1),
        "nrm": jnp.stack([tile(qn_a), tile(qn_b)]),
        "nrmt": jnp.stack([lanes(kn_a), lanes(kn_b)]),
        "gm": jnp.where(blk[:, None] == blk[None, :], 1.0 / HEAD_DIM, 0.0).astype(BF16),
        "wua": w_up_a.astype(BF16), "wub": w_up_b.astype(BF16), "wo": w_out.astype(BF16),
    }


def _pad_rows(a, rows):
    return jnp.pad(a, ((0, 0), (0, rows - a.shape[1])) + ((0, 0),) * (a.ndim - 2))


def _pages_feature_major(cache):
    n_pool = cache.shape[0]
    flat = cache.reshape(n_pool, PAGE, -1)
    return jnp.swapaxes(flat, 1, 2)


def _heads_state(xt, lead):
    b, _, s = xt.shape
    return jnp.transpose(xt.reshape(b, N_HEADS, HEAD_DIM, s), (0, 3, 1, 2)).reshape(*lead, N_HEADS, HEAD_DIM)


def kernel(x_prompt, x_sample, cache_a_k, cache_a_v, cache_a_idx_k, cache_b_k, cache_b_v, cache_b_logf, page_table,
           rel_bias, g_norm, w_in, b_fgate, qnorm_a, knorm_a, qnorm_b, knorm_b, w_up_a, w_up_b, w_out):
    batch, seq, d_model = x_prompt.shape
    bd, t_new, _ = x_sample.shape
    depth, n_pool = cache_a_k.shape[:2]
    n_pages = page_table.shape[1]
    past = n_pages * PAGE
    assert t_new == 1, "the decode kernels handle one new token per sequence"
    assert cache_a_k.shape[2:] == (PAGE, N_HEADS, HEAD_DIM) and bd % LANES == 0

    rel_bias = rel_bias.astype(F32)
    rbt = jnp.pad(rel_bias.T, ((0, QPAD - N_HEADS), (0, 0)))
    tri = jnp.triu(jnp.ones((LANES, LANES), F32)).astype(BF16)

    xp = x_prompt.reshape(batch * seq, d_model)
    xs = x_sample.reshape(bd * t_new, d_model)
    st_p, st_s = [], []
    for layer in range(depth):
        wts = _prepare_weights(d_model, g_norm[layer], w_in[layer], b_fgate[layer], qnorm_a[layer], knorm_a[layer],
                               qnorm_b[layer], knorm_b[layer], w_up_a[layer], w_up_b[layer], w_out[layer])
        pr = _project(xp, wts, batch, seq)
        ft = _cumsum_lanes(pr["lft"], tri)
        ub = _fox_prompt(pr, ft, batch, seq)
        ua = _dsa_prompt(pr, rel_bias, tri, batch, seq)
        lead = (batch, seq)
        st_p.append((_heads_state(pr["kat"], lead), _heads_state(pr["vat"], lead), jnp.swapaxes(pr["ikt"], 1, 2),
                     _heads_state(pr["kbt"], lead), _heads_state(pr["vbt"], lead), jnp.swapaxes(pr["lft"], 1, 2)))
        xp = _merge(xp, ua, ub, pr["sga"], pr["sgb"], wts)

        sp = _project(xs, wts, 1, bd)
        rows_of = lambda name: sp[name][0].T
        iw_s, ik16_s, lf_s = sp["iw"], rows_of("ikt16"), rows_of("lft")
        iq3 = _pad_rows(sp["iq16"].reshape(bd, N_HEADS, D_IDX), QPAD)
        iw3 = _pad_rows(iw_s.reshape(bd, N_HEADS, 1), QPAD)
        scores = _idx_sample(iq3, iw3, _pages_feature_major(cache_a_idx_k[layer]), page_table)
        madd, maddn = _select_sample(scores.reshape(bd, past), sp["iq16"], ik16_s, iw_s, tri, t_new)
        ua_s = _dsa_sample(sp, rows_of("kat"), rows_of("vat"), rbt, madd.reshape(bd, n_pages, PAGE), maddn,
                           _pages_feature_major(cache_a_k[layer]), _pages_feature_major(cache_a_v[layer]), page_table)
        fn3 = _pad_rows(lf_s.reshape(bd, N_HEADS, 1), QPAD)
        ub_s = _fox_sample(sp, rows_of("kbt"), rows_of("vbt"), fn3, _pages_feature_major(cache_b_k[layer]),
                           _pages_feature_major(cache_b_v[layer]), _pages_feature_major(cache_b_logf[layer]),
                           page_table)
        lead = (bd, t_new)
        st_s.append((_heads_state(sp["kat"], lead), _heads_state(sp["vat"], lead),
                     rows_of("ikt").reshape(bd, t_new, D_IDX),
                     _heads_state(sp["kbt"], lead), _heads_state(sp["vbt"], lead), lf_s.reshape(bd, t_new, N_HEADS)))
        xs = _merge(xs, ua_s, ub_s, sp["sga"], sp["sgb"], wts)

    outs_p = [jnp.stack(z) for z in zip(*st_p)]
    outs_s = [jnp.stack(z) for z in zip(*st_s)]
    return (xp.reshape(batch, seq, d_model), xs.reshape(bd, t_new, d_model), *outs_p, *outs_s)
```

```python
import functools
import math

import jax
import jax.numpy as jnp
from jax import lax
from jax.experimental import pallas as pl
from jax.experimental.pallas import tpu as pltpu

F32 = jnp.float32
BF16 = jnp.bfloat16
I32 = jnp.int32

HEAD_DIM = 64
N_HEADS = 8
HW = N_HEADS * HEAD_DIM
D_IDX = 64
PAGE = 128
TOPK_MAX = 256
N_BUCKETS = 32
MAX_EXACT = N_BUCKETS // 2
MAX_DISTANCE = 128
FAR_BUCKET = N_BUCKETS - 1
EPS = 1e-6
NEG = -1e30
SCALE = HEAD_DIM ** -0.5
IDX_W_SCALE = (N_HEADS * D_IDX) ** -0.5

LANES = 128
ATTN_PAGES = 32
IDX_PAGES = 32
QPAD = 16
BISECT_TRIPS = N_HEADS
CAUSAL_STEP = 256
HEADS_PER_TRIP = 4
VMEM_LIMIT = 56 * 1024 * 1024

NT = (((1,), (1,)), ((), ()))


def _cparams(n_axes, vmem=None):
    return pltpu.CompilerParams(dimension_semantics=("arbitrary",) * n_axes, vmem_limit_bytes=vmem)


def _log_sigmoid(z):
    return -(jnp.maximum(-z, 0.0) + jnp.log1p(jnp.exp(-jnp.abs(z))))


def _t5_bucket(dist):
    d = jnp.maximum(dist, 0)
    df = jnp.maximum(d, 1).astype(F32)
    large = MAX_EXACT + (jnp.log(df / MAX_EXACT) / math.log(MAX_DISTANCE / MAX_EXACT)
                         * (N_BUCKETS - MAX_EXACT)).astype(I32)
    large = jnp.minimum(large, N_BUCKETS - 1)
    return jnp.where(d < MAX_EXACT, d, large)


def _head_rows(h):
    if isinstance(h, int):
        return slice(h * HEAD_DIM, (h + 1) * HEAD_DIM)
    return pl.ds(pl.multiple_of(h * HEAD_DIM, HEAD_DIM), HEAD_DIM)


def _causal_widths(seq):
    step = CAUSAL_STEP if seq % CAUSAL_STEP == 0 else seq
    return step, [step * (t + 1) for t in range(seq // step)]


def _project_body(x_ref, g_ref, wr_ref, wt_ref, bft_ref, nrm_ref, nrmt_ref, gm_ref,
                  qa_o, sza_o, iq_o, iw_o, qb_o, szb_o, sga_o, sgb_o,
                  kat_o, vat_o, kbt_o, vbt_o, kat16_o, vat16_o, kbt16_o, vbt16_o, ikt_o, ikt16_o, lft_o,
                  *, d_model, tm):
    x = x_ref[...]
    ms = jnp.mean(x * x, axis=-1, keepdims=True)
    h = ((x * lax.rsqrt(ms + EPS)) * g_ref[...]).astype(BF16)
    gm = gm_ref[...]
    nd = d_model // HW

    def rows(i, n=HW):
        return lax.dot_general(h, wr_ref[i * HW:i * HW + n, :], NT, preferred_element_type=F32)

    def cols(i, n=HW):
        return lax.dot_general(wt_ref[i * HW:i * HW + n, :], h, NT, preferred_element_type=F32)

    def head_norm(p, row):
        ss = jnp.dot((p * p).astype(BF16), gm, preferred_element_type=F32)
        return (p * lax.rsqrt(ss + EPS)) * nrm_ref[row:row + 1, :]

    def head_norm_t(pt, j):
        ss = jnp.dot(gm, (pt * pt).astype(BF16), preferred_element_type=F32)
        return (pt * lax.rsqrt(ss + EPS)) * jnp.concatenate([nrmt_ref[j]] * (tm // LANES), axis=1)

    def silu(z):
        return z * jax.nn.sigmoid(z)

    qa_o[...] = (head_norm(rows(0), 0) * SCALE).astype(BF16)
    sza_o[...] = silu(rows(1)).astype(BF16)
    iq_o[...] = rows(2).astype(BF16)
    qb_o[...] = (head_norm(rows(3), 1) * SCALE).astype(BF16)
    szb_o[...] = silu(rows(4)).astype(BF16)
    for c in range(nd):
        sga_o[:, c * HW:(c + 1) * HW] = jax.nn.sigmoid(rows(5 + c)).astype(BF16)
        sgb_o[:, c * HW:(c + 1) * HW] = jax.nn.sigmoid(rows(5 + nd + c)).astype(BF16)
    iw_o[...] = rows(5 + 2 * nd, LANES)[:, 0:N_HEADS] * IDX_W_SCALE

    kat = head_norm_t(cols(0), 0)
    kat_o[...] = kat
    kat16_o[...] = kat.astype(BF16)
    vat = cols(1)
    vat_o[...] = vat
    vat16_o[...] = vat.astype(BF16)
    kbt = head_norm_t(cols(2), 1)
    kbt_o[...] = kbt
    kbt16_o[...] = kbt.astype(BF16)
    vbt = cols(3)
    vbt_o[...] = vbt
    vbt16_o[...] = vbt.astype(BF16)
    small = cols(4, LANES)
    ikt = small[0:D_IDX]
    ikt_o[...] = ikt
    ikt16_o[...] = ikt.astype(BF16)
    lft_o[...] = _log_sigmoid(small[D_IDX:D_IDX + N_HEADS] + bft_ref[...])


def _project(x2d, wts, batch, seq):
    rows, d_model = x2d.shape
    assert rows == batch * seq
    tm = min(512, seq)
    assert seq % tm == 0 and tm % LANES == 0 and d_model % HW == 0
    nt = seq // tm
    consts = (wts["g"], wts["wr"], wts["wt"], wts["bft"], wts["nrm"], wts["nrmt"], wts["gm"])

    def row(n, dt):
        return jax.ShapeDtypeStruct((rows, n), dt), pl.BlockSpec((tm, n), lambda b, i: (b * nt + i, 0))

    def col(n, dt):
        return (jax.ShapeDtypeStruct((batch, n, seq), dt), pl.BlockSpec((None, n, tm), lambda b, i: (b, 0, i)))

    outs = [row(HW, BF16), row(HW, BF16), row(HW, BF16), row(N_HEADS, F32), row(HW, BF16), row(HW, BF16),
            row(d_model, BF16), row(d_model, BF16),
            col(HW, F32), col(HW, F32), col(HW, F32), col(HW, F32),
            col(HW, BF16), col(HW, BF16), col(HW, BF16), col(HW, BF16),
            col(D_IDX, F32), col(D_IDX, BF16), col(N_HEADS, F32)]
    names = ("qa16", "sza", "iq16", "iw", "qb16", "szb", "sga", "sgb",
             "kat", "vat", "kbt", "vbt", "kat16", "vat16", "kbt16", "vbt16", "ikt", "ikt16", "lft")
    res = pl.pallas_call(
        functools.partial(_project_body, d_model=d_model, tm=tm),
        grid=(batch, nt),
        in_specs=[pl.BlockSpec((tm, d_model), lambda b, i: (b * nt + i, 0))]
        + [pl.BlockSpec(c.shape, lambda b, i, n=c.ndim: (0,) * n, pipeline_mode=pl.Buffered(1)) for c in consts],
        out_specs=[o[1] for o in outs],
        out_shape=[o[0] for o in outs],
        compiler_params=_cparams(2, VMEM_LIMIT),
        name="project",
    )(x2d, *consts)
    return dict(zip(names, res))


def _split3(a):
    hi = a.astype(BF16)
    r1 = a - hi.astype(F32)
    mid = r1.astype(BF16)
    lo = (r1 - mid.astype(F32)).astype(BF16)
    return hi, mid, lo


def _cumsum_body(x_ref, tri_ref, o_ref, *, seq):
    tri = tri_ref[...]
    carry = jnp.zeros((x_ref.shape[0], 1), F32)
    for c in range(seq // LANES):
        hi, mid, lo = _split3(x_ref[:, c * LANES:(c + 1) * LANES])
        cs = (jnp.dot(hi, tri, preferred_element_type=F32) + jnp.dot(mid, tri, preferred_element_type=F32)
              + jnp.dot(lo, tri, preferred_element_type=F32)) + carry
        o_ref[:, c * LANES:(c + 1) * LANES] = cs
        carry = cs[:, LANES - 1:LANES]


def _cumsum_lanes(xt, tri):
    batch, n, seq = xt.shape
    spec = pl.BlockSpec((None, n, seq), lambda b: (b, 0, 0))
    return pl.pallas_call(
        functools.partial(_cumsum_body, seq=seq),
        grid=(batch,),
        in_specs=[spec, pl.BlockSpec(tri.shape, lambda b: (0, 0))],
        out_specs=spec,
        out_shape=jax.ShapeDtypeStruct(xt.shape, F32),
        compiler_params=_cparams(1),
        name="cumsum",
    )(xt, tri)


def _fox_prompt_body(q_ref, kt_ref, vt_ref, ft_ref, sz_ref, o_ref, q8_ref, o8_ref, *, tq, step, widths):
    i = pl.program_id(1)
    for h in range(N_HEADS):
        q8_ref[h] = q_ref[:, _head_rows(h)]
    tier = ((i + 1) * tq + step - 1) // step - 1

    for t, w in enumerate(widths):
        @pl.when(tier == t)
        def _(w=w):
            row_pos = i * tq + lax.broadcasted_iota(I32, (tq, step), 0)
            col_pos = (w - step) + lax.broadcasted_iota(I32, (tq, step), 1)
            visible = col_pos <= row_pos

            def head(h):
                hr = _head_rows(h)
                s = jnp.dot(q8_ref[h], kt_ref[hr, :w], preferred_element_type=F32) - ft_ref[pl.ds(h, 1), :w]
                tail = jnp.where(visible, s[:, w - step:], NEG)
                s = tail if w == step else jnp.concatenate([s[:, :w - step], tail], axis=1)
                p = jnp.exp(s - jnp.max(s, axis=-1, keepdims=True))
                den = jnp.sum(p, axis=-1, keepdims=True)
                o8_ref[h] = lax.dot_general(p.astype(BF16), vt_ref[hr, :w], NT, preferred_element_type=F32) / den

            def group(j, carry):
                for u in range(HEADS_PER_TRIP):
                    head(HEADS_PER_TRIP * j + u)
                return carry

            lax.fori_loop(0, N_HEADS // HEADS_PER_TRIP, group, 0)

    for h in range(N_HEADS):
        hs = _head_rows(h)
        o_ref[:, hs] = (o8_ref[h] * sz_ref[:, hs].astype(F32)).astype(BF16)


def _fox_prompt(pr, ft, batch, seq):
    tq = min(256, seq)
    nq = seq // tq
    step, widths = _causal_widths(seq)
    assert step % tq == 0
    rowspec = pl.BlockSpec((tq, HW), lambda b, i: (b * nq + i, 0))
    seqspec = lambda n: pl.BlockSpec((None, n, seq), lambda b, i: (b, 0, 0))
    return pl.pallas_call(
        functools.partial(_fox_prompt_body, tq=tq, step=step, widths=widths),
        grid=(batch, nq),
        in_specs=[rowspec, seqspec(HW), seqspec(HW), seqspec(N_HEADS), rowspec],
        out_specs=rowspec,
        out_shape=jax.ShapeDtypeStruct((batch * seq, HW), BF16),
        scratch_shapes=[pltpu.VMEM((N_HEADS, tq, HEAD_DIM), BF16), pltpu.VMEM((N_HEADS, tq, HEAD_DIM), F32)],
        compiler_params=_cparams(2, VMEM_LIMIT),
        name="fox_prompt",
    )(pr["qb16"], pr["kbt16"], pr["vbt16"], ft, pr["szb"])


def _count(pred):
    return jnp.sum(jnp.where(pred, 1.0, 0.0), axis=-1, keepdims=True)


def _kth_largest(x_ref, extra, kf, lo0, hi0, side_work=None):
    def cnt_gt(v):
        c = _count(x_ref[...] > v)
        return c if extra is None else c + jnp.where(extra > v, 1.0, 0.0)

    def cnt_ge(v):
        c = _count(x_ref[...] >= v)
        return c if extra is None else c + jnp.where(extra >= v, 1.0, 0.0)

    def max_where(pred_fn):
        x = x_ref[...]
        m = jnp.max(jnp.where(pred_fn(x), x, -jnp.inf), axis=-1, keepdims=True)
        return m if extra is None else jnp.maximum(m, jnp.where(pred_fn(extra), extra, -jnp.inf))

    def bisect(_, c):
        lo, hi = c
        mid = 0.5 * (lo + hi)
        above = cnt_gt(mid) >= kf
        return jnp.where(above, mid, lo), jnp.where(above, hi, mid)

    def trip(j, c):
        if side_work is not None:
            side_work(j)
        return bisect(j, bisect(j, c))

    lo, hi = lax.fori_loop(0, BISECT_TRIPS, trip, (lo0, hi0))

    def finish(c):
        lo, hi, _ = c
        lo, hi = lax.fori_loop(0, 4, bisect, (lo, hi), unroll=True)
        t1 = max_where(lambda x: x <= hi)
        ok = cnt_ge(t1) >= kf
        t2 = max_where(lambda x: x < t1)
        pending = jnp.sum(jnp.where(ok, 0.0, 1.0))
        return jnp.where(ok, t1, lo), jnp.where(ok, t1, t2), pending

    _, hi, _ = lax.while_loop(lambda c: c[2] > 0.0, finish, (lo, hi, jnp.float32(1.0)))
    return hi


def _topk_mask(x_ref, madd_ref, tri_ref, extra, kf, thr):
    x = x_ref[...]
    c_gt = _count(x > thr)
    c_ge = _count(x >= thr)
    if extra is not None:
        c_gt = c_gt + jnp.where(extra > thr, 1.0, 0.0)
        c_ge = c_ge + jnp.where(extra >= thr, 1.0, 0.0)
    need = kf - c_gt
    madd_ref[...] = jnp.where(x >= thr, 0.0, NEG)
    surplus = jnp.sum(jnp.where(c_ge > kf, 1.0, 0.0))

    @pl.when(surplus > 0.0)
    def _():
        tri = tri_ref[...]
        seen = jnp.zeros_like(thr)
        for c in range(x_ref.shape[1] // LANES):
            cs = slice(c * LANES, (c + 1) * LANES)
            xc = x_ref[:, cs]
            tie = jnp.where(xc == thr, 1.0, 0.0)
            rank = jnp.dot(tie.astype(BF16), tri, preferred_element_type=F32) + seen
            madd_ref[:, cs] = jnp.where(xc > thr, 0.0, jnp.where(xc == thr, jnp.where(rank <= need, 0.0, NEG), NEG))
            seen = rank[:, LANES - 1:LANES]

    if extra is None:
        return None
    ties_before = c_ge - c_gt - jnp.where(extra == thr, 1.0, 0.0)
    return jnp.where(extra > thr, 0.0, jnp.where(extra == thr, jnp.where(ties_before + 1.0 <= need, 0.0, NEG), NEG))


def _dsa_prompt_body(rb_ref, q_ref, iq_ref, iw_ref, sz_ref, ikt_ref, kt_ref, vt_ref, tri_ref, o_ref,
                     tb_ref, sc_ref, madd_ref, q8_ref, o8_ref, qk_ref, *, topk, tq, step, widths):
    b = pl.program_id(0)
    i = pl.program_id(1)
    per_step = step // tq
    tail_w = step + tq

    @pl.when((b == 0) & (i == 0))
    def _tail_bias():
        ii = lax.broadcasted_iota(I32, (tq, tq), 0)
        jj = lax.broadcasted_iota(I32, (tq, tq), 1)
        tb_ref[...] = jnp.zeros(tb_ref.shape, F32)
        for t in range(2):
            bucket = _t5_bucket(ii - jj + tq * t)
            for h in range(N_HEADS):
                acc = jnp.zeros((tq, tq), F32)
                for kb in range(N_BUCKETS):
                    acc = jnp.where(bucket == kb, rb_ref[kb, h], acc)
                acc = acc - rb_ref[FAR_BUCKET, h]
                for r in range(per_step):
                    c = r + 1 - t
                    tb_ref[r, h, :, c * tq:(c + 1) * tq] = acc

    for h in range(N_HEADS):
        q8_ref[h] = q_ref[:, _head_rows(h)]
    pos_q = i * tq + lax.broadcasted_iota(I32, (tq, 1), 0)
    kf = jnp.minimum(topk, pos_q + 1).astype(F32)
    r_in_step = i % per_step
    tier = ((i + 1) * tq + step - 1) // step - 1

    for t, w in enumerate(widths):
        @pl.when(tier == t)
        def _(w=w):
            x_ref = sc_ref.at[:, pl.ds(0, w)]
            m_ref = madd_ref.at[:, pl.ds(0, w)]
            adm = lax.broadcasted_iota(I32, (tq, w), 1) <= pos_q
            sc = jnp.zeros((tq, w), F32)
            ikt = ikt_ref[:, :w]
            for h in range(N_HEADS):
                r = jnp.dot(iq_ref[:, h * D_IDX:(h + 1) * D_IDX], ikt, preferred_element_type=F32)
                sc = sc + jnp.maximum(r, 0.0) * iw_ref[:, h:h + 1]
            sc = sc + 0.0
            x_ref[...] = jnp.where(adm, sc, -jnp.inf)
            lo0 = jnp.min(jnp.where(adm, sc, jnp.inf), axis=-1, keepdims=True)
            hi0 = jnp.max(jnp.where(adm, sc, -jnp.inf), axis=-1, keepdims=True)
            def head_logits(h):
                qk_ref[h, :, pl.ds(0, w)] = jnp.dot(q8_ref[h], kt_ref[_head_rows(h), :w],
                                                    preferred_element_type=F32)

            thr = _kth_largest(x_ref, None, kf, lo0, hi0, side_work=head_logits)
            _topk_mask(x_ref, m_ref, tri_ref, None, kf, thr)

            tail = min(w, tail_w)

            def head(h):
                hr = _head_rows(h)
                lg = qk_ref[h, :, pl.ds(0, w)] + m_ref[...]
                near = lg[:, w - tail:] + tb_ref[r_in_step, h, :, tail_w - tail:]
                lg = near if tail == w else jnp.concatenate([lg[:, :w - tail], near], axis=1)
                p = jnp.exp(lg - jnp.max(lg, axis=-1, keepdims=True))
                den = jnp.sum(p, axis=-1, keepdims=True)
                o8_ref[h] = lax.dot_general(p.astype(BF16), vt_ref[hr, :w], NT, preferred_element_type=F32) / den

            def group(j, carry):
                for u in range(HEADS_PER_TRIP):
                    head(HEADS_PER_TRIP * j + u)
                return carry

            lax.fori_loop(0, N_HEADS // HEADS_PER_TRIP, group, 0)

    for h in range(N_HEADS):
        hs = _head_rows(h)
        o_ref[:, hs] = (o8_ref[h] * sz_ref[:, hs].astype(F32)).astype(BF16)


def _dsa_prompt(pr, rel_bias, tri, batch, seq):
    tq = LANES
    assert seq % tq == 0
    nq = seq // tq
    topk = min(TOPK_MAX, seq // 4)
    step, widths = _causal_widths(seq)
    rowspec = lambda n: pl.BlockSpec((tq, n), lambda b, i: (b * nq + i, 0))
    seqspec = lambda n: pl.BlockSpec((None, n, seq), lambda b, i: (b, 0, 0))
    return pl.pallas_call(
        functools.partial(_dsa_prompt_body, topk=topk, tq=tq, step=step, widths=widths),
        grid=(batch, nq),
        in_specs=[pl.BlockSpec(memory_space=pltpu.SMEM), rowspec(HW), rowspec(HW), rowspec(N_HEADS), rowspec(HW),
                  seqspec(D_IDX), seqspec(HW), seqspec(HW), pl.BlockSpec(tri.shape, lambda b, i: (0, 0))],
        out_specs=rowspec(HW),
        out_shape=jax.ShapeDtypeStruct((batch * seq, HW), BF16),
        scratch_shapes=[pltpu.VMEM((step // tq, N_HEADS, tq, step + tq), F32), pltpu.VMEM((tq, seq), F32),
                        pltpu.VMEM((tq, seq), F32), pltpu.VMEM((N_HEADS, tq, HEAD_DIM), BF16),
                        pltpu.VMEM((N_HEADS, tq, HEAD_DIM), F32), pltpu.VMEM((N_HEADS, tq, seq), F32)],
        compiler_params=_cparams(2, VMEM_LIMIT),
        name="dsa_prompt",
    )(rel_bias, pr["qa16"], pr["iq16"], pr["iw"], pr["sza"], pr["ikt16"], pr["kat16"], pr["vat16"], tri)


def _merge_body(x_ref, ua_ref, ub_ref, sga_ref, sgb_ref, wua_ref, wub_ref, wo_ref, y_ref):
    ya = jnp.dot(ua_ref[...], wua_ref[...], preferred_element_type=F32)
    yb = jnp.dot(ub_ref[...], wub_ref[...], preferred_element_type=F32)
    m = sga_ref[...].astype(F32) * ya + sgb_ref[...].astype(F32) * yb
    y_ref[...] = x_ref[...] + jnp.dot(m.astype(BF16), wo_ref[...], preferred_element_type=F32)


def _merge(x2d, ua, ub, sga, sgb, wts):
    rows, d_model = x2d.shape
    tm = min(512, rows)
    assert rows % tm == 0
    row = lambda n: pl.BlockSpec((tm, n), lambda i: (i, 0))
    full = lambda a: pl.BlockSpec(a.shape, lambda i: (0, 0))
    return pl.pallas_call(
        _merge_body,
        grid=(rows // tm,),
        in_specs=[row(d_model), row(HW), row(HW), row(d_model), row(d_model),
                  full(wts["wua"]), full(wts["wub"]), full(wts["wo"])],
        out_specs=row(d_model),
        out_shape=jax.ShapeDtypeStruct((rows, d_model), F32),
        compiler_params=_cparams(1, VMEM_LIMIT),
        name="merge",
    )(x2d, ua, ub, sga, sgb, wts["wua"], wts["wub"], wts["wo"])


def _idx_sample_body(pt_ref, iq_ref, iw_ref, *refs):
    pages, o_ref = refs[:IDX_PAGES], refs[IDX_PAGES]
    keys = jnp.concatenate([pg[...].astype(BF16) for pg in pages], axis=1)
    r = jnp.dot(iq_ref[...], keys, preferred_element_type=F32)
    s = jnp.sum(jnp.maximum(r, 0.0) * iw_ref[...], axis=0, keepdims=True)
    o_ref[...] = jnp.concatenate([s[:, i * PAGE:(i + 1) * PAGE] for i in range(IDX_PAGES)], axis=0)


def _idx_sample(iq3, iw3, cache_ikt, page_table):
    bd, n_pages = page_table.shape
    assert n_pages % IDX_PAGES == 0
    ng = n_pages // IDX_PAGES
    page_specs = [pl.BlockSpec((None, D_IDX, PAGE), lambda b, g, pt, i=i: (pt[b, g * IDX_PAGES + i], 0, 0))
                  for i in range(IDX_PAGES)]
    return pl.pallas_call(
        _idx_sample_body,
        grid_spec=pltpu.PrefetchScalarGridSpec(
            num_scalar_prefetch=1, grid=(bd, ng),
            in_specs=[pl.BlockSpec((None, QPAD, D_IDX), lambda b, g, pt: (b, 0, 0)),
                      pl.BlockSpec((None, QPAD, 1), lambda b, g, pt: (b, 0, 0))] + page_specs,
            out_specs=pl.BlockSpec((None, IDX_PAGES, PAGE), lambda b, g, pt: (b, g, 0))),
        out_shape=jax.ShapeDtypeStruct((bd, n_pages, PAGE), F32),
        compiler_params=_cparams(2),
        name="idx_sample",
    )(page_table, iq3, iw3, *([cache_ikt] * IDX_PAGES))


def _select_sample_body(sc_ref, iq_ref, ik_ref, iw_ref, tri_ref, madd_ref, maddn_ref, *, topk):
    bd = sc_ref.shape[0]
    ii = lax.broadcasted_iota(I32, (bd, bd), 0)
    jj = lax.broadcasted_iota(I32, (bd, bd), 1)
    ik = ik_ref[...]
    snew = jnp.zeros((bd, 1), F32)
    for h in range(N_HEADS):
        r = lax.dot_general(iq_ref[:, h * D_IDX:(h + 1) * D_IDX], ik, NT, preferred_element_type=F32)
        rd = jnp.sum(jnp.where(ii == jj, r, 0.0), axis=-1, keepdims=True)
        snew = snew + jnp.maximum(rd, 0.0) * iw_ref[:, h:h + 1]
    snew = snew + 0.0
    x = sc_ref[...]
    kf = jnp.full((bd, 1), float(topk), F32)
    lo0 = jnp.minimum(jnp.min(x, axis=-1, keepdims=True), snew)
    hi0 = jnp.maximum(jnp.max(x, axis=-1, keepdims=True), snew)
    thr = _kth_largest(sc_ref, snew, kf, lo0, hi0)
    mn = _topk_mask(sc_ref, madd_ref, tri_ref, snew, kf, thr)
    maddn_ref[...] = jnp.broadcast_to(mn, maddn_ref.shape)


def _select_sample(scores2d, iq16, ik16, iw, tri, n_new):
    bd, past = scores2d.shape
    topk = min(TOPK_MAX, (past + n_new) // 4)
    full = lambda a: pl.BlockSpec(a.shape, lambda i: (0,) * a.ndim)
    return pl.pallas_call(
        functools.partial(_select_sample_body, topk=topk),
        grid=(1,),
        in_specs=[full(scores2d), full(iq16), full(ik16), full(iw), full(tri)],
        out_specs=[pl.BlockSpec((bd, past), lambda i: (0, 0)), pl.BlockSpec((bd, LANES), lambda i: (0, 0))],
        out_shape=[jax.ShapeDtypeStruct((bd, past), F32), jax.ShapeDtypeStruct((bd, LANES), F32)],
        compiler_params=_cparams(1, VMEM_LIMIT),
        name="select_sample",
    )(scores2d, iq16, ik16, iw, tri)


def _head_mask():
    r = lax.broadcasted_iota(I32, (QPAD, HW), 0)
    c = lax.broadcasted_iota(I32, (QPAD, HW), 1)
    return jnp.where((c >= r * HEAD_DIM) & (c < (r + 1) * HEAD_DIM), 1.0, 0.0)


def _online_update(m_ref, l_ref, acc_ref, logits, vt_pages):
    m_old = m_ref[...]
    m_new = jnp.maximum(m_old, jnp.max(logits, axis=-1, keepdims=True))
    p = jnp.exp(logits - m_new)
    corr = jnp.exp(m_old - m_new)
    l_ref[...] = l_ref[...] * corr + jnp.sum(p, axis=-1, keepdims=True)
    pv = jnp.zeros(acc_ref.shape, F32)
    for i, vp in enumerate(vt_pages):
        pv = pv + lax.dot_general(p[:, i * PAGE:(i + 1) * PAGE].astype(BF16), vp[...].astype(BF16), NT,
                                  preferred_element_type=F32)
    acc_ref[...] = acc_ref[...] * corr + pv
    m_ref[...] = m_new


def _finish_new_token(m_ref, l_ref, acc_ref, logit_new, vnew_ref, hmask, sz_ref, o_ref):
    m_old = m_ref[...]
    m_new = jnp.maximum(m_old, logit_new)
    p_new = jnp.exp(logit_new - m_new)
    corr = jnp.exp(m_old - m_new)
    den = l_ref[...] * corr + p_new
    o = (acc_ref[...] * corr + p_new * vnew_ref[...]) / den
    row = jnp.sum(o * hmask, axis=0, keepdims=True)
    o_ref[...] = (row * sz_ref[...].astype(F32)).astype(BF16)


def _dsa_sample_body(pt_ref, rbt_ref, q_ref, knew_ref, vnew_ref, sz_ref, madd_ref, maddn_ref, *refs):
    n = ATTN_PAGES
    kt_pages, vt_pages = refs[:n], refs[n:2 * n]
    o_ref, m_ref, l_ref, acc_ref, blast_ref = refs[2 * n:]
    b = pl.program_id(0)
    g = pl.program_id(1)
    last = g == pl.num_programs(1) - 1
    hmask = _head_mask()
    qbd = (q_ref[...].astype(F32) * hmask).astype(BF16)

    @pl.when((b == 0) & (g == 0))
    def _last_page_bias():
        bucket = _t5_bucket(PAGE - lax.broadcasted_iota(I32, (QPAD, PAGE), 1))
        acc = jnp.zeros((QPAD, PAGE), F32)
        for kb in range(N_BUCKETS):
            acc = jnp.where(bucket == kb, rbt_ref[:, kb:kb + 1], acc)
        blast_ref[...] = acc

    @pl.when(g == 0)
    def _init():
        m_ref[...] = jnp.full(m_ref.shape, NEG, F32)
        l_ref[...] = jnp.zeros(l_ref.shape, F32)
        acc_ref[...] = jnp.zeros(acc_ref.shape, F32)

    far = rbt_ref[:, FAR_BUCKET:FAR_BUCKET + 1]
    parts = []
    for i in range(n):
        lt = jnp.dot(qbd, kt_pages[i][...].astype(BF16), preferred_element_type=F32)
        bias = far if i < n - 1 else jnp.where(last, blast_ref[...], jnp.broadcast_to(far, (QPAD, PAGE)))
        parts.append(lt + madd_ref[i:i + 1, :] + bias)
    _online_update(m_ref, l_ref, acc_ref, jnp.concatenate(parts, axis=1), vt_pages)

    @pl.when(last)
    def _finish():
        ln = jnp.sum(qbd.astype(F32) * knew_ref[...], axis=-1, keepdims=True)
        ln = ln + rbt_ref[:, 0:1] + maddn_ref[:, 0:1]
        _finish_new_token(m_ref, l_ref, acc_ref, ln, vnew_ref, hmask, sz_ref, o_ref)


def _fox_sample_body(pt_ref, q_ref, knew_ref, vnew_ref, sz_ref, fn_ref, *refs):
    n = ATTN_PAGES
    kt_pages, vt_pages, lf_pages = refs[:n], refs[n:2 * n], refs[2 * n:3 * n]
    o_ref, m_ref, l_ref, acc_ref, c_ref = refs[3 * n:]
    g = pl.program_id(1)
    last = g == pl.num_programs(1) - 1
    hmask = _head_mask()
    qbd = (q_ref[...].astype(F32) * hmask).astype(BF16)

    @pl.when(g == 0)
    def _init():
        m_ref[...] = jnp.full(m_ref.shape, NEG, F32)
        l_ref[...] = jnp.zeros(l_ref.shape, F32)
        acc_ref[...] = jnp.zeros(acc_ref.shape, F32)
        c_ref[...] = jnp.zeros(c_ref.shape, F32)

    lf = jnp.concatenate([pg[...] for pg in lf_pages], axis=0)
    lane = lax.broadcasted_iota(I32, lf.shape, 1)
    suf = lf
    k = 1
    while k < PAGE:
        suf = suf + jnp.where(lane + k < PAGE, pltpu.roll(suf, PAGE - k, axis=1), 0.0)
        k *= 2
    later = c_ref[...]
    zpad = jnp.zeros((QPAD - N_HEADS, PAGE), F32)
    parts = []
    for i in range(n):
        rs = slice(i * N_HEADS, (i + 1) * N_HEADS)
        lt = jnp.dot(qbd, kt_pages[i][...].astype(BF16), preferred_element_type=F32)
        bias = (suf[rs] - lf[rs]) + (later + fn_ref[...])[:N_HEADS]
        parts.append(lt + jnp.concatenate([bias, zpad], axis=0))
        later = later + jnp.concatenate([suf[rs, 0:1], zpad[:, 0:1]], axis=0)
    c_ref[...] = later
    _online_update(m_ref, l_ref, acc_ref, jnp.concatenate(parts, axis=1), vt_pages)

    @pl.when(last)
    def _finish():
        ln = jnp.sum(qbd.astype(F32) * knew_ref[...], axis=-1, keepdims=True)
        _finish_new_token(m_ref, l_ref, acc_ref, ln, vnew_ref, hmask, sz_ref, o_ref)


def _row3(a):
    return a.reshape(a.shape[0], 1, a.shape[1])


def _sample_attention(body, name, page_table, small_inputs, small_specs, paged_inputs, paged_specs, bd, ng,
                      extra_scratch):
    rowspec = pl.BlockSpec((None, 1, HW), lambda b, g, pt: (b, 0, 0))
    out = pl.pallas_call(
        body,
        grid_spec=pltpu.PrefetchScalarGridSpec(
            num_scalar_prefetch=1, grid=(bd, ng),
            in_specs=small_specs + paged_specs,
            out_specs=rowspec,
            scratch_shapes=[pltpu.VMEM((QPAD, 1), F32), pltpu.VMEM((QPAD, 1), F32), pltpu.VMEM((QPAD, HW), F32),
                            extra_scratch]),
        out_shape=jax.ShapeDtypeStruct((bd, 1, HW), BF16),
        compiler_params=_cparams(2, VMEM_LIMIT),
        name=name,
    )(page_table, *small_inputs, *paged_inputs)
    return out.reshape(bd, HW)


def _dsa_sample(sp, knew, vnew, rbt, madd3, maddn, cache_kt, cache_vt, page_table):
    bd, n_pages = page_table.shape
    n = ATTN_PAGES
    assert n_pages % n == 0
    ng = n_pages // n
    rowspec = pl.BlockSpec((None, 1, HW), lambda b, g, pt: (b, 0, 0))
    page = lambda i: pl.BlockSpec((None, HW, PAGE), lambda b, g, pt, i=i: (pt[b, g * n + i], 0, 0))
    small_inputs = [rbt, _row3(sp["qa16"]), _row3(knew), _row3(vnew), _row3(sp["sza"]), madd3, _row3(maddn)]
    small_specs = [pl.BlockSpec(rbt.shape, lambda b, g, pt: (0, 0)), rowspec, rowspec, rowspec, rowspec,
                   pl.BlockSpec((None, n, PAGE), lambda b, g, pt: (b, g, 0)),
                   pl.BlockSpec((None, 1, LANES), lambda b, g, pt: (b, 0, 0))]
    return _sample_attention(_dsa_sample_body, "dsa_sample", page_table, small_inputs, small_specs,
                             [cache_kt] * n + [cache_vt] * n, [page(i) for i in range(n)] * 2, bd, ng,
                             pltpu.VMEM((QPAD, PAGE), F32))


def _fox_sample(sp, knew, vnew, fn3, cache_kt, cache_vt, cache_lft, page_table):
    bd, n_pages = page_table.shape
    n = ATTN_PAGES
    assert n_pages % n == 0
    ng = n_pages // n
    rowspec = pl.BlockSpec((None, 1, HW), lambda b, g, pt: (b, 0, 0))
    rev = lambda b, g, pt, i: pt[b, n_pages - 1 - (g * n + i)]
    page = lambda i: pl.BlockSpec((None, HW, PAGE), lambda b, g, pt, i=i: (rev(b, g, pt, i), 0, 0))
    lfpage = lambda i: pl.BlockSpec((None, N_HEADS, PAGE), lambda b, g, pt, i=i: (rev(b, g, pt, i), 0, 0))
    small_inputs = [_row3(sp["qb16"]), _row3(knew), _row3(vnew), _row3(sp["szb"]), fn3]
    small_specs = [rowspec, rowspec, rowspec, rowspec, pl.BlockSpec((None, QPAD, 1), lambda b, g, pt: (b, 0, 0))]
    return _sample_attention(_fox_sample_body, "fox_sample", page_table, small_inputs, small_specs,
                             [cache_kt] * n + [cache_vt] * n + [cache_lft] * n,
                             [page(i) for i in range(n)] * 2 + [lfpage(i) for i in range(n)], bd, ng,
                             pltpu.VMEM((QPAD, 1), F32))


def _prepare_weights(d_model, g_norm, w_in, b_fgate, qn_a, kn_a, qn_b, kn_b, w_up_a, w_up_b, w_out):
    widths = (HW, HW, HW, HW, N_HEADS * D_IDX, N_HEADS, D_IDX, HW, HW, HW, HW, N_HEADS, d_model, d_model)
    offs = [0]
    for w in widths:
        offs.append(offs[-1] + w)
    wt = w_in.T
    seg = lambda a, b: wt[offs[a]:offs[b]]
    zeros = lambda n: jnp.zeros((n, d_model), w_in.dtype)
    w_rows = jnp.concatenate([seg(0, 1), seg(3, 4), seg(4, 5), seg(7, 8), seg(10, 11), seg(12, 14),
                              seg(5, 6), zeros(LANES - N_HEADS)], axis=0)
    w_cols = jnp.concatenate([seg(1, 3), seg(8, 10), seg(6, 7), seg(11, 12),
                              zeros(LANES - D_IDX - N_HEADS)], axis=0)
    tile = lambda v: jnp.tile(v.astype(F32), N_HEADS)
    lanes = lambda v: jnp.broadcast_to(tile(v)[:, None], (HW, LANES))
    blk = jnp.arange(HW) // HEAD_DIM
    return {
        "g": g_norm.astype(F32).reshape(1, d_model),
        "wr": w_rows.astype(BF16), "wt": w_cols.astype(BF16),
        "bft": b_fgate.astype(F32).reshape(N_HEADS, 1),
        "nrm": jnp.stack([tile(qn_a), tile(qn_b)]),
        "nrmt": jnp.stack([lanes(kn_a), lanes(kn_b)]),
        "gm": jnp.where(blk[:, None] == blk[None, :], 1.0 / HEAD_DIM, 0.0).astype(BF16),
        "wua": w_up_a.astype(BF16), "wub": w_up_b.astype(BF16), "wo": w_out.astype(BF16),
    }


def _pad_rows(a, rows):
    return jnp.pad(a, ((0, 0), (0, rows - a.shape[1])) + ((0, 0),) * (a.ndim - 2))


def _pages_feature_major(cache):
    n_pool = cache.shape[0]
    flat = cache.reshape(n_pool, PAGE, -1)
    return jnp.swapaxes(flat, 1, 2)


def _heads_state(xt, lead):
    b, _, s = xt.shape
    return jnp.transpose(xt.reshape(b, N_HEADS, HEAD_DIM, s), (0, 3, 1, 2)).reshape(*lead, N_HEADS, HEAD_DIM)


def kernel(x_prompt, x_sample, cache_a_k, cache_a_v, cache_a_idx_k, cache_b_k, cache_b_v, cache_b_logf, page_table,
           rel_bias, g_norm, w_in, b_fgate, qnorm_a, knorm_a, qnorm_b, knorm_b, w_up_a, w_up_b, w_out):
    batch, seq, d_model = x_prompt.shape
    bd, t_new, _ = x_sample.shape
    depth, n_pool = cache_a_k.shape[:2]
    n_pages = page_table.shape[1]
    past = n_pages * PAGE
    assert t_new == 1, "the decode kernels handle one new token per sequence"
    assert cache_a_k.shape[2:] == (PAGE, N_HEADS, HEAD_DIM) and bd % LANES == 0

    rel_bias = rel_bias.astype(F32)
    rbt = jnp.pad(rel_bias.T, ((0, QPAD - N_HEADS), (0, 0)))
    tri = jnp.triu(jnp.ones((LANES, LANES), F32)).astype(BF16)

    xp = x_prompt.reshape(batch * seq, d_model)
    xs = x_sample.reshape(bd * t_new, d_model)
    st_p, st_s = [], []
    for layer in range(depth):
        wts = _prepare_weights(d_model, g_norm[layer], w_in[layer], b_fgate[layer], qnorm_a[layer], knorm_a[layer],
                               qnorm_b[layer], knorm_b[layer], w_up_a[layer], w_up_b[layer], w_out[layer])
        pr = _project(xp, wts, batch, seq)
        ft = _cumsum_lanes(pr["lft"], tri)
        ub = _fox_prompt(pr, ft, batch, seq)
        ua = _dsa_prompt(pr, rel_bias, tri, batch, seq)
        lead = (batch, seq)
        st_p.append((_heads_state(pr["kat"], lead), _heads_state(pr["vat"], lead), jnp.swapaxes(pr["ikt"], 1, 2),
                     _heads_state(pr["kbt"], lead), _heads_state(pr["vbt"], lead), jnp.swapaxes(pr["lft"], 1, 2)))
        xp = _merge(xp, ua, ub, pr["sga"], pr["sgb"], wts)

        sp = _project(xs, wts, 1, bd)
        rows_of = lambda name: sp[name][0].T
        iw_s, ik16_s, lf_s = sp["iw"], rows_of("ikt16"), rows_of("lft")
        iq3 = _pad_rows(sp["iq16"].reshape(bd, N_HEADS, D_IDX), QPAD)
        iw3 = _pad_rows(iw_s.reshape(bd, N_HEADS, 1), QPAD)
        scores = _idx_sample(iq3, iw3, _pages_feature_major(cache_a_idx_k[layer]), page_table)
        madd, maddn = _select_sample(scores.reshape(bd, past), sp["iq16"], ik16_s, iw_s, tri, t_new)
        ua_s = _dsa_sample(sp, rows_of("kat"), rows_of("vat"), rbt, madd.reshape(bd, n_pages, PAGE), maddn,
                           _pages_feature_major(cache_a_k[layer]), _pages_feature_major(cache_a_v[layer]), page_table)
        fn3 = _pad_rows(lf_s.reshape(bd, N_HEADS, 1), QPAD)
        ub_s = _fox_sample(sp, rows_of("kbt"), rows_of("vbt"), fn3, _pages_feature_major(cache_b_k[layer]),
                           _pages_feature_major(cache_b_v[layer]), _pages_feature_major(cache_b_logf[layer]),
                           page_table)
        lead = (bd, t_new)
        st_s.append((_heads_state(sp["kat"], lead), _heads_state(sp["vat"], lead),
                     rows_of("ikt").reshape(bd, t_new, D_IDX),
                     _heads_state(sp["kbt"], lead), _heads_state(sp["vbt"], lead), lf_s.reshape(bd, t_new, N_HEADS)))
        xs = _merge(xs, ua_s, ub_s, sp["sga"], sp["sgb"], wts)

    outs_p = [jnp.stack(z) for z in zip(*st_p)]
    outs_s = [jnp.stack(z) for z in zip(*st_s)]
    return (xp.reshape(batch, seq, d_model), xs.reshape(bd, t_new, d_model), *outs_p, *outs_s)
```

```python
import functools
import math

import jax
import jax.numpy as jnp
from jax import lax
from jax.experimental import pallas as pl
from jax.experimental.pallas import tpu as pltpu

F32 = jnp.float32
BF16 = jnp.bfloat16
I32 = jnp.int32

HEAD_DIM = 64
N_HEADS = 8
HW = N_HEADS * HEAD_DIM
D_IDX = 64
PAGE = 128
TOPK_MAX = 256
N_BUCKETS = 32
MAX_EXACT = N_BUCKETS // 2
MAX_DISTANCE = 128
FAR_BUCKET = N_BUCKETS - 1
EPS = 1e-6
NEG = -1e30
SCALE = HEAD_DIM ** -0.5
IDX_W_SCALE = (N_HEADS * D_IDX) ** -0.5

LANES = 128
ATTN_PAGES = 32
IDX_PAGES = 64
QPAD = 16
BISECT_TRIPS = N_HEADS
CAUSAL_STEP = 256
HEADS_PER_TRIP = 8
VMEM_LIMIT = 56 * 1024 * 1024

NT = (((1,), (1,)), ((), ()))


def _cparams(n_axes, vmem=None):
    return pltpu.CompilerParams(dimension_semantics=("arbitrary",) * n_axes, vmem_limit_bytes=vmem)


def _log_sigmoid(z):
    return -(jnp.maximum(-z, 0.0) + jnp.log1p(jnp.exp(-jnp.abs(z))))


def _t5_bucket(dist):
    d = jnp.maximum(dist, 0)
    df = jnp.maximum(d, 1).astype(F32)
    large = MAX_EXACT + (jnp.log(df / MAX_EXACT) / math.log(MAX_DISTANCE / MAX_EXACT)
                         * (N_BUCKETS - MAX_EXACT)).astype(I32)
    large = jnp.minimum(large, N_BUCKETS - 1)
    return jnp.where(d < MAX_EXACT, d, large)


def _head_rows(h):
    if isinstance(h, int):
        return slice(h * HEAD_DIM, (h + 1) * HEAD_DIM)
    return pl.ds(pl.multiple_of(h * HEAD_DIM, HEAD_DIM), HEAD_DIM)


def _causal_widths(seq):
    step = CAUSAL_STEP if seq % CAUSAL_STEP == 0 else seq
    return step, [step * (t + 1) for t in range(seq // step)]


def _project_body(x_ref, g_ref, wr_ref, wt_ref, bft_ref, nrm_ref, nrmt_ref, gm_ref,
                  qa_o, sza_o, iq_o, iw_o, qb_o, szb_o, sga_o, sgb_o,
                  kat_o, vat_o, kbt_o, vbt_o, kat16_o, vat16_o, kbt16_o, vbt16_o, ikt_o, ikt16_o, lft_o,
                  *, d_model, tm):
    x = x_ref[...]
    ms = jnp.mean(x * x, axis=-1, keepdims=True)
    h = ((x * lax.rsqrt(ms + EPS)) * g_ref[...]).astype(BF16)
    gm = gm_ref[...]
    nd = d_model // HW

    def rows(i, n=HW):
        return lax.dot_general(h, wr_ref[i * HW:i * HW + n, :], NT, preferred_element_type=F32)

    def cols(i, n=HW):
        return lax.dot_general(wt_ref[i * HW:i * HW + n, :], h, NT, preferred_element_type=F32)

    def head_norm(p, row):
        ss = jnp.dot((p * p).astype(BF16), gm, preferred_element_type=F32)
        return (p * lax.rsqrt(ss + EPS)) * nrm_ref[row:row + 1, :]

    def head_norm_t(pt, j):
        ss = jnp.dot(gm, (pt * pt).astype(BF16), preferred_element_type=F32)
        return (pt * lax.rsqrt(ss + EPS)) * jnp.concatenate([nrmt_ref[j]] * (tm // LANES), axis=1)

    def silu(z):
        return z * jax.nn.sigmoid(z)

    qa_o[...] = (head_norm(rows(0), 0) * SCALE).astype(BF16)
    sza_o[...] = silu(rows(1)).astype(BF16)
    iq_o[...] = rows(2).astype(BF16)
    qb_o[...] = (head_norm(rows(3), 1) * SCALE).astype(BF16)
    szb_o[...] = silu(rows(4)).astype(BF16)
    for c in range(nd):
        sga_o[:, c * HW:(c + 1) * HW] = jax.nn.sigmoid(rows(5 + c)).astype(BF16)
        sgb_o[:, c * HW:(c + 1) * HW] = jax.nn.sigmoid(rows(5 + nd + c)).astype(BF16)
    iw_o[...] = rows(5 + 2 * nd, LANES)[:, 0:N_HEADS] * IDX_W_SCALE

    kat = head_norm_t(cols(0), 0)
    kat_o[...] = kat
    kat16_o[...] = kat.astype(BF16)
    vat = cols(1)
    vat_o[...] = vat
    vat16_o[...] = vat.astype(BF16)
    kbt = head_norm_t(cols(2), 1)
    kbt_o[...] = kbt
    kbt16_o[...] = kbt.astype(BF16)
    vbt = cols(3)
    vbt_o[...] = vbt
    vbt16_o[...] = vbt.astype(BF16)
    small = cols(4, LANES)
    ikt = small[0:D_IDX]
    ikt_o[...] = ikt
    ikt16_o[...] = ikt.astype(BF16)
    lft_o[...] = _log_sigmoid(small[D_IDX:D_IDX + N_HEADS] + bft_ref[...])


def _project(x2d, wts, batch, seq):
    rows, d_model = x2d.shape
    assert rows == batch * seq
    tm = min(512, seq)
    assert seq % tm == 0 and tm % LANES == 0 and d_model % HW == 0
    nt = seq // tm
    consts = (wts["g"], wts["wr"], wts["wt"], wts["bft"], wts["nrm"], wts["nrmt"], wts["gm"])

    def row(n, dt):
        return jax.ShapeDtypeStruct((rows, n), dt), pl.BlockSpec((tm, n), lambda b, i: (b * nt + i, 0))

    def col(n, dt):
        return (jax.ShapeDtypeStruct((batch, n, seq), dt), pl.BlockSpec((None, n, tm), lambda b, i: (b, 0, i)))

    outs = [row(HW, BF16), row(HW, BF16), row(HW, BF16), row(N_HEADS, F32), row(HW, BF16), row(HW, BF16),
            row(d_model, BF16), row(d_model, BF16),
            col(HW, F32), col(HW, F32), col(HW, F32), col(HW, F32),
            col(HW, BF16), col(HW, BF16), col(HW, BF16), col(HW, BF16),
            col(D_IDX, F32), col(D_IDX, BF16), col(N_HEADS, F32)]
    names = ("qa16", "sza", "iq16", "iw", "qb16", "szb", "sga", "sgb",
             "kat", "vat", "kbt", "vbt", "kat16", "vat16", "kbt16", "vbt16", "ikt", "ikt16", "lft")
    res = pl.pallas_call(
        functools.partial(_project_body, d_model=d_model, tm=tm),
        grid=(batch, nt),
        in_specs=[pl.BlockSpec((tm, d_model), lambda b, i: (b * nt + i, 0))]
        + [pl.BlockSpec(c.shape, lambda b, i, n=c.ndim: (0,) * n, pipeline_mode=pl.Buffered(1)) for c in consts],
        out_specs=[o[1] for o in outs],
        out_shape=[o[0] for o in outs],
        compiler_params=_cparams(2, VMEM_LIMIT),
        name="project",
    )(x2d, *consts)
    return dict(zip(names, res))


def _split3(a):
    hi = a.astype(BF16)
    r1 = a - hi.astype(F32)
    mid = r1.astype(BF16)
    lo = (r1 - mid.astype(F32)).astype(BF16)
    return hi, mid, lo


def _cumsum_body(x_ref, tri_ref, o_ref, *, seq):
    tri = tri_ref[...]
    carry = jnp.zeros((x_ref.shape[0], 1), F32)
    for c in range(seq // LANES):
        hi, mid, lo = _split3(x_ref[:, c * LANES:(c + 1) * LANES])
        cs = (jnp.dot(hi, tri, preferred_element_type=F32) + jnp.dot(mid, tri, preferred_element_type=F32)
              + jnp.dot(lo, tri, preferred_element_type=F32)) + carry
        o_ref[:, c * LANES:(c + 1) * LANES] = cs
        carry = cs[:, LANES - 1:LANES]


def _cumsum_lanes(xt, tri):
    batch, n, seq = xt.shape
    spec = pl.BlockSpec((None, n, seq), lambda b: (b, 0, 0))
    return pl.pallas_call(
        functools.partial(_cumsum_body, seq=seq),
        grid=(batch,),
        in_specs=[spec, pl.BlockSpec(tri.shape, lambda b: (0, 0))],
        out_specs=spec,
        out_shape=jax.ShapeDtypeStruct(xt.shape, F32),
        compiler_params=_cparams(1),
        name="cumsum",
    )(xt, tri)


def _fox_prompt_body(q_ref, kt_ref, vt_ref, ft_ref, sz_ref, o_ref, q8_ref, o8_ref, *, tq, step, widths):
    i = pl.program_id(1)
    for h in range(N_HEADS):
        q8_ref[h] = q_ref[:, _head_rows(h)]
    tier = ((i + 1) * tq + step - 1) // step - 1

    for t, w in enumerate(widths):
        @pl.when(tier == t)
        def _(w=w):
            row_pos = i * tq + lax.broadcasted_iota(I32, (tq, step), 0)
            col_pos = (w - step) + lax.broadcasted_iota(I32, (tq, step), 1)
            visible = col_pos <= row_pos

            def head(h):
                hr = _head_rows(h)
                s = jnp.dot(q8_ref[h], kt_ref[hr, :w], preferred_element_type=F32) - ft_ref[pl.ds(h, 1), :w]
                tail = jnp.where(visible, s[:, w - step:], NEG)
                s = tail if w == step else jnp.concatenate([s[:, :w - step], tail], axis=1)
                p = jnp.exp(s - jnp.max(s, axis=-1, keepdims=True))
                den = jnp.sum(p, axis=-1, keepdims=True)
                o8_ref[h] = lax.dot_general(p.astype(BF16), vt_ref[hr, :w], NT, preferred_element_type=F32) / den

            def group(j, carry):
                for u in range(HEADS_PER_TRIP):
                    head(HEADS_PER_TRIP * j + u)
                return carry

            lax.fori_loop(0, N_HEADS // HEADS_PER_TRIP, group, 0)

    for h in range(N_HEADS):
        hs = _head_rows(h)
        o_ref[:, hs] = (o8_ref[h] * sz_ref[:, hs].astype(F32)).astype(BF16)


def _fox_prompt(pr, ft, batch, seq):
    tq = min(256, seq)
    nq = seq // tq
    step, widths = _causal_widths(seq)
    assert step % tq == 0
    rowspec = pl.BlockSpec((tq, HW), lambda b, i: (b * nq + i, 0))
    seqspec = lambda n: pl.BlockSpec((None, n, seq), lambda b, i: (b, 0, 0))
    return pl.pallas_call(
        functools.partial(_fox_prompt_body, tq=tq, step=step, widths=widths),
        grid=(batch, nq),
        in_specs=[rowspec, seqspec(HW), seqspec(HW), seqspec(N_HEADS), rowspec],
        out_specs=rowspec,
        out_shape=jax.ShapeDtypeStruct((batch * seq, HW), BF16),
        scratch_shapes=[pltpu.VMEM((N_HEADS, tq, HEAD_DIM), BF16), pltpu.VMEM((N_HEADS, tq, HEAD_DIM), F32)],
        compiler_params=_cparams(2, VMEM_LIMIT),
        name="fox_prompt",
    )(pr["qb16"], pr["kbt16"], pr["vbt16"], ft, pr["szb"])


def _count(pred):
    return jnp.sum(jnp.where(pred, 1.0, 0.0), axis=-1, keepdims=True)


def _kth_largest(x_ref, extra, kf, lo0, hi0, side_work=None):
    def cnt_gt(v):
        c = _count(x_ref[...] > v)
        return c if extra is None else c + jnp.where(extra > v, 1.0, 0.0)

    def cnt_ge(v):
        c = _count(x_ref[...] >= v)
        return c if extra is None else c + jnp.where(extra >= v, 1.0, 0.0)

    def max_where(pred_fn):
        x = x_ref[...]
        m = jnp.max(jnp.where(pred_fn(x), x, -jnp.inf), axis=-1, keepdims=True)
        return m if extra is None else jnp.maximum(m, jnp.where(pred_fn(extra), extra, -jnp.inf))

    def bisect(_, c):
        lo, hi = c
        mid = 0.5 * (lo + hi)
        above = cnt_gt(mid) >= kf
        return jnp.where(above, mid, lo), jnp.where(above, hi, mid)

    def trip(j, c):
        if side_work is not None:
            side_work(j)
        return bisect(j, bisect(j, c))

    lo, hi = lax.fori_loop(0, BISECT_TRIPS, trip, (lo0, hi0))

    def finish(c):
        lo, hi, _ = c
        lo, hi = lax.fori_loop(0, 4, bisect, (lo, hi), unroll=True)
        t1 = max_where(lambda x: x <= hi)
        ok = cnt_ge(t1) >= kf
        t2 = max_where(lambda x: x < t1)
        pending = jnp.sum(jnp.where(ok, 0.0, 1.0))
        return jnp.where(ok, t1, lo), jnp.where(ok, t1, t2), pending

    _, hi, _ = lax.while_loop(lambda c: c[2] > 0.0, finish, (lo, hi, jnp.float32(1.0)))
    return hi


def _topk_mask(x_ref, madd_ref, tri_ref, extra, kf, thr):
    x = x_ref[...]
    c_gt = _count(x > thr)
    c_ge = _count(x >= thr)
    if extra is not None:
        c_gt = c_gt + jnp.where(extra > thr, 1.0, 0.0)
        c_ge = c_ge + jnp.where(extra >= thr, 1.0, 0.0)
    need = kf - c_gt
    madd_ref[...] = jnp.where(x >= thr, 0.0, NEG)
    surplus = jnp.sum(jnp.where(c_ge > kf, 1.0, 0.0))

    @pl.when(surplus > 0.0)
    def _():
        tri = tri_ref[...]
        seen = jnp.zeros_like(thr)
        for c in range(x_ref.shape[1] // LANES):
            cs = slice(c * LANES, (c + 1) * LANES)
            xc = x_ref[:, cs]
            tie = jnp.where(xc == thr, 1.0, 0.0)
            rank = jnp.dot(tie.astype(BF16), tri, preferred_element_type=F32) + seen
            madd_ref[:, cs] = jnp.where(xc > thr, 0.0, jnp.where(xc == thr, jnp.where(rank <= need, 0.0, NEG), NEG))
            seen = rank[:, LANES - 1:LANES]

    if extra is None:
        return None
    ties_before = c_ge - c_gt - jnp.where(extra == thr, 1.0, 0.0)
    return jnp.where(extra > thr, 0.0, jnp.where(extra == thr, jnp.where(ties_before + 1.0 <= need, 0.0, NEG), NEG))


def _dsa_prompt_body(rb_ref, q_ref, iq_ref, iw_ref, sz_ref, ikt_ref, kt_ref, vt_ref, tri_ref, o_ref,
                     tb_ref, sc_ref, madd_ref, q8_ref, o8_ref, qk_ref, *, topk, tq, step, widths):
    b = pl.program_id(0)
    i = pl.program_id(1)
    per_step = step // tq
    tail_w = step + tq

    @pl.when((b == 0) & (i == 0))
    def _tail_bias():
        ii = lax.broadcasted_iota(I32, (tq, tq), 0)
        jj = lax.broadcasted_iota(I32, (tq, tq), 1)
        tb_ref[...] = jnp.zeros(tb_ref.shape, F32)
        for t in range(2):
            bucket = _t5_bucket(ii - jj + tq * t)
            for h in range(N_HEADS):
                acc = jnp.zeros((tq, tq), F32)
                for kb in range(N_BUCKETS):
                    acc = jnp.where(bucket == kb, rb_ref[kb, h], acc)
                acc = acc - rb_ref[FAR_BUCKET, h]
                for r in range(per_step):
                    c = r + 1 - t
                    tb_ref[r, h, :, c * tq:(c + 1) * tq] = acc

    for h in range(N_HEADS):
        q8_ref[h] = q_ref[:, _head_rows(h)]
    pos_q = i * tq + lax.broadcasted_iota(I32, (tq, 1), 0)
    kf = jnp.minimum(topk, pos_q + 1).astype(F32)
    r_in_step = i % per_step
    tier = ((i + 1) * tq + step - 1) // step - 1

    for t, w in enumerate(widths):
        @pl.when(tier == t)
        def _(w=w):
            x_ref = sc_ref.at[:, pl.ds(0, w)]
            m_ref = madd_ref.at[:, pl.ds(0, w)]
            adm = lax.broadcasted_iota(I32, (tq, w), 1) <= pos_q
            sc = jnp.zeros((tq, w), F32)
            ikt = ikt_ref[:, :w]
            for h in range(N_HEADS):
                r = jnp.dot(iq_ref[:, h * D_IDX:(h + 1) * D_IDX], ikt, preferred_element_type=F32)
                sc = sc + jnp.maximum(r, 0.0) * iw_ref[:, h:h + 1]
            sc = sc + 0.0
            x_ref[...] = jnp.where(adm, sc, -jnp.inf)
            lo0 = jnp.min(jnp.where(adm, sc, jnp.inf), axis=-1, keepdims=True)
            hi0 = jnp.max(jnp.where(adm, sc, -jnp.inf), axis=-1, keepdims=True)
            def head_logits(h):
                qk_ref[h, :, pl.ds(0, w)] = jnp.dot(q8_ref[h], kt_ref[_head_rows(h), :w],
                                                    preferred_element_type=F32)

            thr = _kth_largest(x_ref, None, kf, lo0, hi0, side_work=head_logits)
            _topk_mask(x_ref, m_ref, tri_ref, None, kf, thr)

            tail = min(w, tail_w)

            def head(h):
                hr = _head_rows(h)
                lg = qk_ref[h, :, pl.ds(0, w)] + m_ref[...]
                near = lg[:, w - tail:] + tb_ref[r_in_step, h, :, tail_w - tail:]
                lg = near if tail == w else jnp.concatenate([lg[:, :w - tail], near], axis=1)
                p = jnp.exp(lg - jnp.max(lg, axis=-1, keepdims=True))
                den = jnp.sum(p, axis=-1, keepdims=True)
                o8_ref[h] = lax.dot_general(p.astype(BF16), vt_ref[hr, :w], NT, preferred_element_type=F32) / den

            def group(j, carry):
                for u in range(HEADS_PER_TRIP):
                    head(HEADS_PER_TRIP * j + u)
                return carry

            lax.fori_loop(0, N_HEADS // HEADS_PER_TRIP, group, 0)

    for h in range(N_HEADS):
        hs = _head_rows(h)
        o_ref[:, hs] = (o8_ref[h] * sz_ref[:, hs].astype(F32)).astype(BF16)


def _dsa_prompt(pr, rel_bias, tri, batch, seq):
    tq = LANES
    assert seq % tq == 0
    nq = seq // tq
    topk = min(TOPK_MAX, seq // 4)
    step, widths = _causal_widths(seq)
    rowspec = lambda n: pl.BlockSpec((tq, n), lambda b, i: (b * nq + i, 0))
    seqspec = lambda n: pl.BlockSpec((None, n, seq), lambda b, i: (b, 0, 0))
    return pl.pallas_call(
        functools.partial(_dsa_prompt_body, topk=topk, tq=tq, step=step, widths=widths),
        grid=(batch, nq),
        in_specs=[pl.BlockSpec(memory_space=pltpu.SMEM), rowspec(HW), rowspec(HW), rowspec(N_HEADS), rowspec(HW),
                  seqspec(D_IDX), seqspec(HW), seqspec(HW), pl.BlockSpec(tri.shape, lambda b, i: (0, 0))],
        out_specs=rowspec(HW),
        out_shape=jax.ShapeDtypeStruct((batch * seq, HW), BF16),
        scratch_shapes=[pltpu.VMEM((step // tq, N_HEADS, tq, step + tq), F32), pltpu.VMEM((tq, seq), F32),
                        pltpu.VMEM((tq, seq), F32), pltpu.VMEM((N_HEADS, tq, HEAD_DIM), BF16),
                        pltpu.VMEM((N_HEADS, tq, HEAD_DIM), F32), pltpu.VMEM((N_HEADS, tq, seq), F32)],
        compiler_params=_cparams(2, VMEM_LIMIT),
        name="dsa_prompt",
    )(rel_bias, pr["qa16"], pr["iq16"], pr["iw"], pr["sza"], pr["ikt16"], pr["kat16"], pr["vat16"], tri)


def _merge_body(x_ref, ua_ref, ub_ref, sga_ref, sgb_ref, wua_ref, wub_ref, wo_ref, y_ref):
    ya = jnp.dot(ua_ref[...], wua_ref[...], preferred_element_type=F32)
    yb = jnp.dot(ub_ref[...], wub_ref[...], preferred_element_type=F32)
    m = sga_ref[...].astype(F32) * ya + sgb_ref[...].astype(F32) * yb
    y_ref[...] = x_ref[...] + jnp.dot(m.astype(BF16), wo_ref[...], preferred_element_type=F32)


def _merge(x2d, ua, ub, sga, sgb, wts):
    rows, d_model = x2d.shape
    tm = min(512, rows)
    assert rows % tm == 0
    row = lambda n: pl.BlockSpec((tm, n), lambda i: (i, 0))
    full = lambda a: pl.BlockSpec(a.shape, lambda i: (0, 0))
    return pl.pallas_call(
        _merge_body,
        grid=(rows // tm,),
        in_specs=[row(d_model), row(HW), row(HW), row(d_model), row(d_model),
                  full(wts["wua"]), full(wts["wub"]), full(wts["wo"])],
        out_specs=row(d_model),
        out_shape=jax.ShapeDtypeStruct((rows, d_model), F32),
        compiler_params=_cparams(1, VMEM_LIMIT),
        name="merge",
    )(x2d, ua, ub, sga, sgb, wts["wua"], wts["wub"], wts["wo"])


def _idx_sample_body(pt_ref, iq_ref, iw_ref, *refs, n):
    pages, o_ref = refs[:n], refs[n]
    keys = jnp.concatenate([pg[...].astype(BF16) for pg in pages], axis=1)
    r = jnp.dot(iq_ref[...], keys, preferred_element_type=F32)
    s = jnp.sum(jnp.maximum(r, 0.0) * iw_ref[...], axis=0, keepdims=True)
    o_ref[...] = jnp.concatenate([s[:, i * PAGE:(i + 1) * PAGE] for i in range(n)], axis=0)


def _idx_sample(iq3, iw3, cache_ikt, page_table):
    bd, n_pages = page_table.shape
    n = math.gcd(n_pages, IDX_PAGES)
    assert n % 8 == 0
    ng = n_pages // n
    page_specs = [pl.BlockSpec((None, D_IDX, PAGE), lambda b, g, pt, i=i: (pt[b, g * n + i], 0, 0))
                  for i in range(n)]
    return pl.pallas_call(
        functools.partial(_idx_sample_body, n=n),
        grid_spec=pltpu.PrefetchScalarGridSpec(
            num_scalar_prefetch=1, grid=(bd, ng),
            in_specs=[pl.BlockSpec((None, QPAD, D_IDX), lambda b, g, pt: (b, 0, 0)),
                      pl.BlockSpec((None, QPAD, 1), lambda b, g, pt: (b, 0, 0))] + page_specs,
            out_specs=pl.BlockSpec((None, n, PAGE), lambda b, g, pt: (b, g, 0))),
        out_shape=jax.ShapeDtypeStruct((bd, n_pages, PAGE), F32),
        compiler_params=_cparams(2),
        name="idx_sample",
    )(page_table, iq3, iw3, *([cache_ikt] * n))


def _select_sample_body(sc_ref, iq_ref, ik_ref, iw_ref, tri_ref, madd_ref, maddn_ref, *, topk):
    bd = sc_ref.shape[0]
    ii = lax.broadcasted_iota(I32, (bd, bd), 0)
    jj = lax.broadcasted_iota(I32, (bd, bd), 1)
    ik = ik_ref[...]
    snew = jnp.zeros((bd, 1), F32)
    for h in range(N_HEADS):
        r = lax.dot_general(iq_ref[:, h * D_IDX:(h + 1) * D_IDX], ik, NT, preferred_element_type=F32)
        rd = jnp.sum(jnp.where(ii == jj, r, 0.0), axis=-1, keepdims=True)
        snew = snew + jnp.maximum(rd, 0.0) * iw_ref[:, h:h + 1]
    snew = snew + 0.0
    x = sc_ref[...]
    kf = jnp.full((bd, 1), float(topk), F32)
    lo0 = jnp.minimum(jnp.min(x, axis=-1, keepdims=True), snew)
    hi0 = jnp.maximum(jnp.max(x, axis=-1, keepdims=True), snew)
    thr = _kth_largest(sc_ref, snew, kf, lo0, hi0)
    mn = _topk_mask(sc_ref, madd_ref, tri_ref, snew, kf, thr)
    maddn_ref[...] = jnp.broadcast_to(mn, maddn_ref.shape)


def _select_sample(scores2d, iq16, ik16, iw, tri, n_new):
    bd, past = scores2d.shape
    topk = min(TOPK_MAX, (past + n_new) // 4)
    full = lambda a: pl.BlockSpec(a.shape, lambda i: (0,) * a.ndim)
    return pl.pallas_call(
        functools.partial(_select_sample_body, topk=topk),
        grid=(1,),
        in_specs=[full(scores2d), full(iq16), full(ik16), full(iw), full(tri)],
        out_specs=[pl.BlockSpec((bd, past), lambda i: (0, 0)), pl.BlockSpec((bd, LANES), lambda i: (0, 0))],
        out_shape=[jax.ShapeDtypeStruct((bd, past), F32), jax.ShapeDtypeStruct((bd, LANES), F32)],
        compiler_params=_cparams(1, VMEM_LIMIT),
        name="select_sample",
    )(scores2d, iq16, ik16, iw, tri)


def _head_mask():
    r = lax.broadcasted_iota(I32, (QPAD, HW), 0)
    c = lax.broadcasted_iota(I32, (QPAD, HW), 1)
    return jnp.where((c >= r * HEAD_DIM) & (c < (r + 1) * HEAD_DIM), 1.0, 0.0)


def _online_update(m_ref, l_ref, acc_ref, logits, vt_pages):
    m_old = m_ref[...]
    m_new = jnp.maximum(m_old, jnp.max(logits, axis=-1, keepdims=True))
    p = jnp.exp(logits - m_new)
    corr = jnp.exp(m_old - m_new)
    l_ref[...] = l_ref[...] * corr + jnp.sum(p, axis=-1, keepdims=True)
    pv = jnp.zeros(acc_ref.shape, F32)
    for i, vp in enumerate(vt_pages):
        pv = pv + lax.dot_general(p[:, i * PAGE:(i + 1) * PAGE].astype(BF16), vp[...].astype(BF16), NT,
                                  preferred_element_type=F32)
    acc_ref[...] = acc_ref[...] * corr + pv
    m_ref[...] = m_new


def _finish_new_token(m_ref, l_ref, acc_ref, logit_new, vnew_ref, hmask, sz_ref, o_ref):
    m_old = m_ref[...]
    m_new = jnp.maximum(m_old, logit_new)
    p_new = jnp.exp(logit_new - m_new)
    corr = jnp.exp(m_old - m_new)
    den = l_ref[...] * corr + p_new
    o = (acc_ref[...] * corr + p_new * vnew_ref[...]) / den
    row = jnp.sum(o * hmask, axis=0, keepdims=True)
    o_ref[...] = (row * sz_ref[...].astype(F32)).astype(BF16)


def _dsa_sample_body(pt_ref, rbt_ref, q_ref, knew_ref, vnew_ref, sz_ref, madd_ref, maddn_ref, *refs):
    n = ATTN_PAGES
    kt_pages, vt_pages = refs[:n], refs[n:2 * n]
    o_ref, m_ref, l_ref, acc_ref, blast_ref = refs[2 * n:]
    b = pl.program_id(0)
    g = pl.program_id(1)
    last = g == pl.num_programs(1) - 1
    hmask = _head_mask()
    qbd = (q_ref[...].astype(F32) * hmask).astype(BF16)

    @pl.when((b == 0) & (g == 0))
    def _last_page_bias():
        bucket = _t5_bucket(PAGE - lax.broadcasted_iota(I32, (QPAD, PAGE), 1))
        acc = jnp.zeros((QPAD, PAGE), F32)
        for kb in range(N_BUCKETS):
            acc = jnp.where(bucket == kb, rbt_ref[:, kb:kb + 1], acc)
        blast_ref[...] = acc

    @pl.when(g == 0)
    def _init():
        m_ref[...] = jnp.full(m_ref.shape, NEG, F32)
        l_ref[...] = jnp.zeros(l_ref.shape, F32)
        acc_ref[...] = jnp.zeros(acc_ref.shape, F32)

    far = rbt_ref[:, FAR_BUCKET:FAR_BUCKET + 1]
    parts = []
    for i in range(n):
        lt = jnp.dot(qbd, kt_pages[i][...].astype(BF16), preferred_element_type=F32)
        bias = far if i < n - 1 else jnp.where(last, blast_ref[...], jnp.broadcast_to(far, (QPAD, PAGE)))
        parts.append(lt + madd_ref[i:i + 1, :] + bias)
    _online_update(m_ref, l_ref, acc_ref, jnp.concatenate(parts, axis=1), vt_pages)

    @pl.when(last)
    def _finish():
        ln = jnp.sum(qbd.astype(F32) * knew_ref[...], axis=-1, keepdims=True)
        ln = ln + rbt_ref[:, 0:1] + maddn_ref[:, 0:1]
        _finish_new_token(m_ref, l_ref, acc_ref, ln, vnew_ref, hmask, sz_ref, o_ref)


def _fox_sample_body(pt_ref, q_ref, knew_ref, vnew_ref, sz_ref, fn_ref, *refs):
    n = ATTN_PAGES
    kt_pages, vt_pages, lf_pages = refs[:n], refs[n:2 * n], refs[2 * n:3 * n]
    o_ref, m_ref, l_ref, acc_ref, c_ref = refs[3 * n:]
    g = pl.program_id(1)
    last = g == pl.num_programs(1) - 1
    hmask = _head_mask()
    qbd = (q_ref[...].astype(F32) * hmask).astype(BF16)

    @pl.when(g == 0)
    def _init():
        m_ref[...] = jnp.full(m_ref.shape, NEG, F32)
        l_ref[...] = jnp.zeros(l_ref.shape, F32)
        acc_ref[...] = jnp.zeros(acc_ref.shape, F32)
        c_ref[...] = jnp.zeros(c_ref.shape, F32)

    lf = jnp.concatenate([pg[...] for pg in lf_pages], axis=0)
    lane = lax.broadcasted_iota(I32, lf.shape, 1)
    suf = lf
    k = 1
    while k < PAGE:
        suf = suf + jnp.where(lane + k < PAGE, pltpu.roll(suf, PAGE - k, axis=1), 0.0)
        k *= 2
    later = c_ref[...]
    zpad = jnp.zeros((QPAD - N_HEADS, PAGE), F32)
    parts = []
    for i in range(n):
        rs = slice(i * N_HEADS, (i + 1) * N_HEADS)
        lt = jnp.dot(qbd, kt_pages[i][...].astype(BF16), preferred_element_type=F32)
        bias = (suf[rs] - lf[rs]) + (later + fn_ref[...])[:N_HEADS]
        parts.append(lt + jnp.concatenate([bias, zpad], axis=0))
        later = later + jnp.concatenate([suf[rs, 0:1], zpad[:, 0:1]], axis=0)
    c_ref[...] = later
    _online_update(m_ref, l_ref, acc_ref, jnp.concatenate(parts, axis=1), vt_pages)

    @pl.when(last)
    def _finish():
        ln = jnp.sum(qbd.astype(F32) * knew_ref[...], axis=-1, keepdims=True)
        _finish_new_token(m_ref, l_ref, acc_ref, ln, vnew_ref, hmask, sz_ref, o_ref)


def _row3(a):
    return a.reshape(a.shape[0], 1, a.shape[1])


def _sample_attention(body, name, page_table, small_inputs, small_specs, paged_inputs, paged_specs, bd, ng,
                      extra_scratch):
    rowspec = pl.BlockSpec((None, 1, HW), lambda b, g, pt: (b, 0, 0))
    out = pl.pallas_call(
        body,
        grid_spec=pltpu.PrefetchScalarGridSpec(
            num_scalar_prefetch=1, grid=(bd, ng),
            in_specs=small_specs + paged_specs,
            out_specs=rowspec,
            scratch_shapes=[pltpu.VMEM((QPAD, 1), F32), pltpu.VMEM((QPAD, 1), F32), pltpu.VMEM((QPAD, HW), F32),
                            extra_scratch]),
        out_shape=jax.ShapeDtypeStruct((bd, 1, HW), BF16),
        compiler_params=_cparams(2, VMEM_LIMIT),
        name=name,
    )(page_table, *small_inputs, *paged_inputs)
    return out.reshape(bd, HW)


def _dsa_sample(sp, knew, vnew, rbt, madd3, maddn, cache_kt, cache_vt, page_table):
    bd, n_pages = page_table.shape
    n = ATTN_PAGES
    assert n_pages % n == 0
    ng = n_pages // n
    rowspec = pl.BlockSpec((None, 1, HW), lambda b, g, pt: (b, 0, 0))
    page = lambda i: pl.BlockSpec((None, HW, PAGE), lambda b, g, pt, i=i: (pt[b, g * n + i], 0, 0))
    small_inputs = [rbt, _row3(sp["qa16"]), _row3(knew), _row3(vnew), _row3(sp["sza"]), madd3, _row3(maddn)]
    small_specs = [pl.BlockSpec(rbt.shape, lambda b, g, pt: (0, 0)), rowspec, rowspec, rowspec, rowspec,
                   pl.BlockSpec((None, n, PAGE), lambda b, g, pt: (b, g, 0)),
                   pl.BlockSpec((None, 1, LANES), lambda b, g, pt: (b, 0, 0))]
    return _sample_attention(_dsa_sample_body, "dsa_sample", page_table, small_inputs, small_specs,
                             [cache_kt] * n + [cache_vt] * n, [page(i) for i in range(n)] * 2, bd, ng,
                             pltpu.VMEM((QPAD, PAGE), F32))


def _fox_sample(sp, knew, vnew, fn3, cache_kt, cache_vt, cache_lft, page_table):
    bd, n_pages = page_table.shape
    n = ATTN_PAGES
    assert n_pages % n == 0
    ng = n_pages // n
    rowspec = pl.BlockSpec((None, 1, HW), lambda b, g, pt: (b, 0, 0))
    rev = lambda b, g, pt, i: pt[b, n_pages - 1 - (g * n + i)]
    page = lambda i: pl.BlockSpec((None, HW, PAGE), lambda b, g, pt, i=i: (rev(b, g, pt, i), 0, 0))
    lfpage = lambda i: pl.BlockSpec((None, N_HEADS, PAGE), lambda b, g, pt, i=i: (rev(b, g, pt, i), 0, 0))
    small_inputs = [_row3(sp["qb16"]), _row3(knew), _row3(vnew), _row3(sp["szb"]), fn3]
    small_specs = [rowspec, rowspec, rowspec, rowspec, pl.BlockSpec((None, QPAD, 1), lambda b, g, pt: (b, 0, 0))]
    return _sample_attention(_fox_sample_body, "fox_sample", page_table, small_inputs, small_specs,
                             [cache_kt] * n + [cache_vt] * n + [cache_lft] * n,
                             [page(i) for i in range(n)] * 2 + [lfpage(i) for i in range(n)], bd, ng,
                             pltpu.VMEM((QPAD, 1), F32))


def _prepare_weights(d_model, g_norm, w_in, b_fgate, qn_a, kn_a, qn_b, kn_b, w_up_a, w_up_b, w_out):
    widths = (HW, HW, HW, HW, N_HEADS * D_IDX, N_HEADS, D_IDX, HW, HW, HW, HW, N_HEADS, d_model, d_model)
    offs = [0]
    for w in widths:
        offs.append(offs[-1] + w)
    wt = w_in.T
    seg = lambda a, b: wt[offs[a]:offs[b]]
    zeros = lambda n: jnp.zeros((n, d_model), w_in.dtype)
    w_rows = jnp.concatenate([seg(0, 1), seg(3, 4), seg(4, 5), seg(7, 8), seg(10, 11), seg(12, 14),
                              seg(5, 6), zeros(LANES - N_HEADS)], axis=0)
    w_cols = jnp.concatenate([seg(1, 3), seg(8, 10), seg(6, 7), seg(11, 12),
                              zeros(LANES - D_IDX - N_HEADS)], axis=0)
    tile = lambda v: jnp.tile(v.astype(F32), N_HEADS)
    lanes = lambda v: jnp.broadcast_to(tile(v)[:, None], (HW, LANES))
    blk = jnp.arange(HW) // HEAD_DIM
    return {
        "g": g_norm.astype(F32).reshape(1, d_model),
        "wr": w_rows.astype(BF16), "wt": w_cols.astype(BF16),
        "bft": b_fgate.astype(F32).reshape(N_HEADS, 1),
        "nrm": jnp.stack([tile(qn_a), tile(qn_b)]),
        "nrmt": jnp.stack([lanes(kn_a), lanes(kn_b)]),
        "gm": jnp.where(blk[:, None] == blk[None, :], 1.0 / HEAD_DIM, 0.0).astype(BF16),
        "wua": w_up_a.astype(BF16), "wub": w_up_b.astype(BF16), "wo": w_out.astype(BF16),
    }


def _pad_rows(a, rows):
    return jnp.pad(a, ((0, 0), (0, rows - a.shape[1])) + ((0, 0),) * (a.ndim - 2))


def _pages_feature_major(cache):
    n_pool = cache.shape[0]
    flat = cache.reshape(n_pool, PAGE, -1)
    return jnp.swapaxes(flat, 1, 2)


def _heads_state(xt, lead):
    b, _, s = xt.shape
    return jnp.transpose(xt.reshape(b, N_HEADS, HEAD_DIM, s), (0, 3, 1, 2)).reshape(*lead, N_HEADS, HEAD_DIM)


def kernel(x_prompt, x_sample, cache_a_k, cache_a_v, cache_a_idx_k, cache_b_k, cache_b_v, cache_b_logf, page_table,
           rel_bias, g_norm, w_in, b_fgate, qnorm_a, knorm_a, qnorm_b, knorm_b, w_up_a, w_up_b, w_out):
    batch, seq, d_model = x_prompt.shape
    bd, t_new, _ = x_sample.shape
    depth, n_pool = cache_a_k.shape[:2]
    n_pages = page_table.shape[1]
    past = n_pages * PAGE
    assert t_new == 1, "the decode kernels handle one new token per sequence"
    assert cache_a_k.shape[2:] == (PAGE, N_HEADS, HEAD_DIM) and bd % LANES == 0

    rel_bias = rel_bias.astype(F32)
    rbt = jnp.pad(rel_bias.T, ((0, QPAD - N_HEADS), (0, 0)))
    tri = jnp.triu(jnp.ones((LANES, LANES), F32)).astype(BF16)

    xp = x_prompt.reshape(batch * seq, d_model)
    xs = x_sample.reshape(bd * t_new, d_model)
    st_p, st_s = [], []
    for layer in range(depth):
        wts = _prepare_weights(d_model, g_norm[layer], w_in[layer], b_fgate[layer], qnorm_a[layer], knorm_a[layer],
                               qnorm_b[layer], knorm_b[layer], w_up_a[layer], w_up_b[layer], w_out[layer])
        pr = _project(xp, wts, batch, seq)
        ft = _cumsum_lanes(pr["lft"], tri)
        ub = _fox_prompt(pr, ft, batch, seq)
        ua = _dsa_prompt(pr, rel_bias, tri, batch, seq)
        lead = (batch, seq)
        st_p.append((_heads_state(pr["kat"], lead), _heads_state(pr["vat"], lead), jnp.swapaxes(pr["ikt"], 1, 2),
                     _heads_state(pr["kbt"], lead), _heads_state(pr["vbt"], lead), jnp.swapaxes(pr["lft"], 1, 2)))
        xp = _merge(xp, ua, ub, pr["sga"], pr["sgb"], wts)

        sp = _project(xs, wts, 1, bd)
        rows_of = lambda name: sp[name][0].T
        iw_s, ik16_s, lf_s = sp["iw"], rows_of("ikt16"), rows_of("lft")
        iq3 = _pad_rows(sp["iq16"].reshape(bd, N_HEADS, D_IDX), QPAD)
        iw3 = _pad_rows(iw_s.reshape(bd, N_HEADS, 1), QPAD)
        scores = _idx_sample(iq3, iw3, _pages_feature_major(cache_a_idx_k[layer]), page_table)
        madd, maddn = _select_sample(scores.reshape(bd, past), sp["iq16"], ik16_s, iw_s, tri, t_new)
        ua_s = _dsa_sample(sp, rows_of("kat"), rows_of("vat"), rbt, madd.reshape(bd, n_pages, PAGE), maddn,
                           _pages_feature_major(cache_a_k[layer]), _pages_feature_major(cache_a_v[layer]), page_table)
        fn3 = _pad_rows(lf_s.reshape(bd, N_HEADS, 1), QPAD)
        ub_s = _fox_sample(sp, rows_of("kbt"), rows_of("vbt"), fn3, _pages_feature_major(cache_b_k[layer]),
                           _pages_feature_major(cache_b_v[layer]), _pages_feature_major(cache_b_logf[layer]),
                           page_table)
        lead = (bd, t_new)
        st_s.append((_heads_state(sp["kat"], lead), _heads_state(sp["vat"], lead),
                     rows_of("ikt").reshape(bd, t_new, D_IDX),
                     _heads_state(sp["kbt"], lead), _heads_state(sp["vbt"], lead), lf_s.reshape(bd, t_new, N_HEADS)))
        xs = _merge(xs, ua_s, ub_s, sp["sga"], sp["sgb"], wts)

    outs_p = [jnp.stack(z) for z in zip(*st_p)]
    outs_s = [jnp.stack(z) for z in zip(*st_s)]
    return (xp.reshape(batch, seq, d_model), xs.reshape(bd, t_new, d_model), *outs_p, *outs_s)
```
